```python
import math
import jax, jax.numpy as jnp
from jax import lax
import numpy as np

D_MODEL = 1024
BATCH = 4
SEQ = 4096
DEPTH = 4
DEC_BATCH = 32
DEC_SEQ = 8
PAST_LEN = 8192
PAGE_SIZE = 128

D_MIX = D_MODEL
HEAD_DIM = 64
N_HEADS_TOTAL = D_MIX // HEAD_DIM
HA = N_HEADS_TOTAL // 2
HB = N_HEADS_TOTAL - HA
DA = HA * HEAD_DIM
DB = HB * HEAD_DIM
N_IN = 2 * DA + 3 * DB + HB
SPLITS = (DA, 2 * DA, 2 * DA + DB, 2 * DA + 2 * DB, 2 * DA + 3 * DB)
CHUNK = 128
Q_BLOCK = 128
N_EXPERTS = 32
TOP_K = 4
D_FF = D_MODEL
SWIGLU_ALPHA = 1.702
SWIGLU_LIMIT = 7.0
MOE_BLOCK = 128
EPS = 1e-6
FORGET_BIAS_INIT = 4.0

kernel_name = "hymba_chunkmlp_fox_moe_step"


def _rms(x, g):
    xf = x.astype(jnp.float32)
    y = xf * lax.rsqrt(jnp.mean(xf * xf, axis=-1, keepdims=True) + EPS)
    return (y * g.astype(jnp.float32)).astype(x.dtype)


def _project(x, norm_g, w_in, b_f, gv_g, q_g, k_g):
    B, L = x.shape[:2]
    h = _rms(x, norm_g)
    p = h @ w_in
    ua, va, qb, kb, vb, fl = jnp.split(p, SPLITS, axis=-1)
    u = jax.nn.gelu(ua).reshape(B, L, HA, HEAD_DIM)
    va = _rms(jax.nn.gelu(va).reshape(B, L, HA, HEAD_DIM), gv_g)
    q = _rms(qb.reshape(B, L, HB, HEAD_DIM), q_g)
    k = _rms(kb.reshape(B, L, HB, HEAD_DIM), k_g)
    v = vb.reshape(B, L, HB, HEAD_DIM)
    logf = jax.nn.log_sigmoid(fl.astype(jnp.float32) + b_f.astype(jnp.float32))
    return u, va, q, k, v, logf


def _chunk_mlp(u, v, w_s, b_s):
    B, L = u.shape[:2]
    pad = (-L) % CHUNK
    vp = jnp.pad(v, ((0, 0), (0, pad), (0, 0), (0, 0)))
    n = (L + pad) // CHUNK
    vc = vp.reshape(B, n, CHUNK, HA, HEAD_DIM)
    w = w_s * jnp.tril(jnp.ones((CHUNK, CHUNK), w_s.dtype))
    z = jnp.einsum('hts,bnshd->bnthd', w, vc) + b_s.T[:, :, None]
    z = z.reshape(B, n * CHUNK, HA, HEAD_DIM)[:, :L]
    return u * z


def _fox_prompt(q, k, v, logf):
    B, L = q.shape[:2]
    scale = HEAD_DIM ** -0.5
    c = jnp.cumsum(logf, axis=1)
    cT = c.transpose(0, 2, 1)
    nq = L // Q_BLOCK
    qb = q.reshape(B, nq, Q_BLOCK, HB, HEAD_DIM).transpose(1, 0, 2, 3, 4)
    cb = cT.reshape(B, HB, nq, Q_BLOCK).transpose(2, 0, 1, 3)
    kpos = jnp.arange(L)

    def block(args):
        qi, ci, i = args
        s = jnp.einsum('bqhd,bkhd->bhqk', qi, k).astype(jnp.float32) * scale
        s = s + ci[..., :, None] - cT[..., None, :]
        qpos = i * Q_BLOCK + jnp.arange(Q_BLOCK)
        mask = kpos[None, :] <= qpos[:, None]
        s = jnp.where(mask, s, -jnp.inf)
        p = jax.nn.softmax(s, axis=-1)
        return jnp.einsum('bhqk,bkhd->bqhd', p.astype(v.dtype), v)

    o = lax.map(block, (qb, cb, jnp.arange(nq)))
    return o.transpose(1, 0, 2, 3, 4).reshape(B, L, HB, HEAD_DIM)


def _fox_sample(q, k, v, logf, k_past, v_past, logf_past):
    S = q.shape[1]
    scale = HEAD_DIM ** -0.5
    cn = jnp.cumsum(logf, axis=1).transpose(0, 2, 1)
    lp = logf_past.astype(jnp.float32)
    dpast = (lax.cumsum(lp, axis=1, reverse=True) - lp).transpose(0, 2, 1)
    s_past = jnp.einsum('bqhd,bkhd->bhqk', q, k_past).astype(jnp.float32) * scale
    s_past = s_past + cn[..., :, None] + dpast[..., None, :]
    s_new = jnp.einsum('bqhd,bkhd->bhqk', q, k).astype(jnp.float32) * scale
    s_new = s_new + cn[..., :, None] - cn[..., None, :]
    causal = jnp.arange(S)[None, :] <= jnp.arange(S)[:, None]
    s_new = jnp.where(causal, s_new, -jnp.inf)
    p = jax.nn.softmax(jnp.concatenate([s_past, s_new], axis=-1), axis=-1).astype(v.dtype)
    P = k_past.shape[1]
    return (jnp.einsum('bhqk,bkhd->bqhd', p[..., :P], v_past)
            + jnp.einsum('bhqk,bkhd->bqhd', p[..., P:], v))


def _moe(h, w_router, b_router, w_gu, b_gu, w_dn, b_dn):
    T, D = h.shape
    logits = (h @ w_router).astype(jnp.float32) + b_router.astype(jnp.float32)
    top_val, top_idx = lax.top_k(logits, TOP_K)
    gates = jax.nn.softmax(top_val, axis=-1)
    M = T * TOP_K
    e_flat = top_idx.reshape(M)
    tok_flat = jnp.repeat(jnp.arange(T, dtype=jnp.int32), TOP_K)
    g_flat = gates.reshape(M)
    order = jnp.argsort(e_flat, stable=True)
    e_s, tok_s, g_s = e_flat[order], tok_flat[order], g_flat[order]
    sizes = jnp.bincount(e_flat, length=N_EXPERTS)
    start = jnp.cumsum(sizes) - sizes
    psizes = (sizes + MOE_BLOCK - 1) // MOE_BLOCK * MOE_BLOCK
    pend = jnp.cumsum(psizes)
    pstart = pend - psizes
    dest = pstart[e_s] + jnp.arange(M) - start[e_s]
    n_blocks = -(-(M + N_EXPERTS * (MOE_BLOCK - 1)) // MOE_BLOCK)
    n_slots = n_blocks * MOE_BLOCK
    slot_tok = jnp.full((n_slots,), T, jnp.int32).at[dest].set(tok_s)
    h_pad = jnp.concatenate([h, jnp.zeros((1, D), h.dtype)], axis=0)
    xs = h_pad[slot_tok].reshape(n_blocks, MOE_BLOCK, D)
    block_e = jnp.minimum(jnp.searchsorted(pend, jnp.arange(n_blocks) * MOE_BLOCK, side='right'),
                          N_EXPERTS - 1)

    def expert_block(args):
        xb, e = args
        gu = xb @ w_gu[e] + b_gu[e]
        glu = jnp.minimum(gu[:, :D_FF], SWIGLU_LIMIT)
        lin = jnp.clip(gu[:, D_FF:], -SWIGLU_LIMIT, SWIGLU_LIMIT)
        act = glu * jax.nn.sigmoid(SWIGLU_ALPHA * glu) * (lin + 1)
        return act @ w_dn[e] + b_dn[e]

    ys = lax.map(expert_block, (xs, block_e)).reshape(n_slots, D)
    y_assign = ys[dest] * g_s[:, None].astype(ys.dtype)
    return jax.ops.segment_sum(y_assign, tok_s, num_segments=T)


def _finish(x, out_a, out_b, w_out, norm2_g, w_router, b_router, w_gu, b_gu, w_dn, b_dn):
    B, L = x.shape[:2]
    mix = jnp.concatenate([out_a.reshape(B, L, DA), out_b.reshape(B, L, DB)], axis=-1) @ w_out
    x = x + mix
    h2 = _rms(x, norm2_g).reshape(B * L, D_MODEL)
    return x + _moe(h2, w_router, b_router, w_gu, b_gu, w_dn, b_dn).reshape(B, L, D_MODEL)


def setup_inputs(seed: int = 0) -> dict:
    key = jax.random.key(seed)
    ks = jax.random.split(key, 24)
    n_pages = PAST_LEN // PAGE_SIZE
    n_used = DEC_BATCH * n_pages
    n_phys = n_used + max(1, n_used // 4)

    def nrm(k, shape, scale):
        return jax.random.normal(k, shape, jnp.float32) * scale

    return {
        "x_prompt": nrm(ks[0], (BATCH, SEQ, D_MODEL), 1.0),
        "x_sample": nrm(ks[1], (DEC_BATCH, DEC_SEQ, D_MODEL), 1.0),
        "cache_k": nrm(ks[2], (DEPTH, n_phys, PAGE_SIZE, HB, HEAD_DIM), 1.0),
        "cache_v": nrm(ks[3], (DEPTH, n_phys, PAGE_SIZE, HB, HEAD_DIM), 1.0),
        "cache_logf": jax.nn.log_sigmoid(nrm(ks[4], (DEPTH, n_phys, PAGE_SIZE, HB), 1.0) + FORGET_BIAS_INIT),
        "page_table": jax.random.permutation(ks[5], n_phys)[:n_used].reshape(DEC_BATCH, n_pages).astype(jnp.int32),
        "norm1_g": 1.0 + nrm(ks[6], (DEPTH, D_MODEL), 0.02),
        "w_in": nrm(ks[7], (DEPTH, D_MODEL, N_IN), D_MODEL ** -0.5),
        "b_f": FORGET_BIAS_INIT + nrm(ks[8], (DEPTH, HB), 0.1),
        "gv_g": 1.0 + nrm(ks[9], (DEPTH, HA, HEAD_DIM), 0.02),
        "w_s": nrm(ks[10], (DEPTH, HA, CHUNK, CHUNK), CHUNK ** -0.5),
        "b_s": 1.0 + nrm(ks[11], (DEPTH, HA, CHUNK), 0.1),
        "q_g": 1.0 + nrm(ks[12], (DEPTH, HEAD_DIM), 0.02),
        "k_g": 1.0 + nrm(ks[13], (DEPTH, HEAD_DIM), 0.02),
        "w_out": nrm(ks[14], (DEPTH, D_MIX, D_MODEL), 0.5 * D_MIX ** -0.5),
        "norm2_g": 1.0 + nrm(ks[15], (DEPTH, D_MODEL), 0.02),
        "w_router": nrm(ks[16], (DEPTH, D_MODEL, N_EXPERTS), D_MODEL ** -0.5),
        "b_router": nrm(ks[17], (DEPTH, N_EXPERTS), 0.01),
        "w_gu": nrm(ks[18], (DEPTH, N_EXPERTS, D_MODEL, 2 * D_FF), D_MODEL ** -0.5),
        "b_gu": nrm(ks[19], (DEPTH, N_EXPERTS, 2 * D_FF), 0.01),
        "w_dn": nrm(ks[20], (DEPTH, N_EXPERTS, D_FF, D_MODEL), 0.5 * D_FF ** -0.5),
        "b_dn": nrm(ks[21], (DEPTH, N_EXPERTS, D_MODEL), 0.01),
    }


def reference(x_prompt, x_sample, cache_k, cache_v, cache_logf, page_table,
              norm1_g, w_in, b_f, gv_g, w_s, b_s, q_g, k_g, w_out, norm2_g,
              w_router, b_router, w_gu, b_gu, w_dn, b_dn):
    Bd, n_pages = page_table.shape
    P = n_pages * PAGE_SIZE
    xp, xs = x_prompt, x_sample
    kp_l, vp_l, fp_l, ks_l, vs_l, fs_l, cv_l = [], [], [], [], [], [], []
    for l in range(DEPTH):
        u, va, q, k, v, logf = _project(xp, norm1_g[l], w_in[l], b_f[l], gv_g[l], q_g[l], k_g[l])
        out_a = _chunk_mlp(u, va, w_s[l], b_s[l])
        out_b = _fox_prompt(q, k, v, logf)
        xp = _finish(xp, out_a, out_b, w_out[l], norm2_g[l], w_router[l], b_router[l],
                     w_gu[l], b_gu[l], w_dn[l], b_dn[l])
        kp_l.append(k); vp_l.append(v); fp_l.append(logf)
        u, va, q, k, v, logf = _project(xs, norm1_g[l], w_in[l], b_f[l], gv_g[l], q_g[l], k_g[l])
        out_a = _chunk_mlp(u, va, w_s[l], b_s[l])
        k_past = cache_k[l][page_table].reshape(Bd, P, HB, HEAD_DIM)
        v_past = cache_v[l][page_table].reshape(Bd, P, HB, HEAD_DIM)
        f_past = cache_logf[l][page_table].reshape(Bd, P, HB)
        out_b = _fox_sample(q, k, v, logf, k_past, v_past, f_past)
        xs = _finish(xs, out_a, out_b, w_out[l], norm2_g[l], w_router[l], b_router[l],
                     w_gu[l], b_gu[l], w_dn[l], b_dn[l])
        ks_l.append(k); vs_l.append(v); fs_l.append(logf); cv_l.append(va)
    k_prompt = jnp.stack(kp_l)
    v_prompt = jnp.stack(vp_l)
    logf_prompt = jnp.stack(fp_l)
    k_sample = jnp.stack(ks_l)
    v_sample = jnp.stack(vs_l)
    logf_sample = jnp.stack(fs_l)
    chunk_v_sample = jnp.stack(cv_l)
    return (xp, xs, k_prompt, v_prompt, logf_prompt, k_sample, v_sample, logf_sample, chunk_v_sample)
```

```python
import functools
import math

import jax
import jax.numpy as jnp
from jax import lax
from jax.experimental import pallas as pl
from jax.experimental.pallas import tpu as pltpu

F32 = jnp.float32
BF16 = jnp.bfloat16
HIGHEST = lax.Precision.HIGHEST

HEAD_DIM = 64
CHUNK = 128
PAGE = 128
N_EXPERTS = 32
TOP_K = 4
EPS = 1e-6
SWIGLU_ALPHA = 1.702
SWIGLU_LIMIT = 7.0
LOG2E = math.log2(math.e)
NEG_BIG = -1e30

LANES = 128
TM = 256
BQ = 512
BK = 512
MOE_BLOCK = 256
VMEM_LIMIT = 56 * 1024 * 1024


def _cparams(sem):
    return pltpu.CompilerParams(dimension_semantics=sem, vmem_limit_bytes=VMEM_LIMIT)


def _full(shape):
    nd = len(shape)
    return pl.BlockSpec(shape, lambda *_: (0,) * nd)


def _proj_kernel(tiles_per_seq,
                 x_ref, g1_ref, wm_ref, wf_ref, bf_ref, gv_ref, qg_ref, kg_ref, bd_ref,
                 wblk_ref, bias_ref, ltri_ref,
                 oa_ref, qa_ref, ka_ref, vb_ref, k32_ref, v32_ref, logf_ref, c_ref,
                 va_ref, qs_ref, carry_ref):
    i = pl.program_id(0)
    da = gv_ref.shape[1]
    db = qg_ref.shape[1]
    x = x_ref[...]
    ms = jnp.mean(x * x, axis=-1, keepdims=True)
    h = (x * lax.rsqrt(ms + EPS) * g1_ref[...]).astype(BF16)
    p = jnp.dot(h, wm_ref[...], preferred_element_type=F32)
    fl = jnp.dot(h, wf_ref[...], preferred_element_type=F32)

    def group_norm(t, gamma):
        sq = (t * t).astype(BF16)
        gms = jnp.dot(sq, bd_ref[...], preferred_element_type=F32) * (1.0 / HEAD_DIM)
        return t * lax.rsqrt(gms + EPS) * gamma

    u = jax.nn.gelu(p[:, 0:da])
    van = group_norm(jax.nn.gelu(p[:, da:2 * da]), gv_ref[...])
    qn = group_norm(p[:, 2 * da:2 * da + db], qg_ref[...])
    kn = group_norm(p[:, 2 * da + db:2 * da + 2 * db], kg_ref[...])
    vv = p[:, 2 * da + 2 * db:2 * da + 3 * db]

    va_ref[...] = van
    qs_ref[...] = qn
    k32_ref[...] = kn
    v32_ref[...] = vv
    vb_ref[...] = vv.astype(BF16)

    lane = lax.broadcasted_iota(jnp.int32, (x.shape[0], LANES), 1)
    low = lane < HEAD_DIM

    van_bf = van.astype(BF16)
    for j in range(da // LANES):
        rhs = van_bf[:, j * LANES:(j + 1) * LANES]
        z0 = jnp.dot(wblk_ref[0, 2 * j], rhs, preferred_element_type=F32)
        z1 = jnp.dot(wblk_ref[0, 2 * j + 1], rhs, preferred_element_type=F32)
        z = jnp.where(low, z0, z1) + bias_ref[0, :, j * LANES:(j + 1) * LANES]
        oa_ref[:, j * LANES:(j + 1) * LANES] = (u[:, j * LANES:(j + 1) * LANES] * z).astype(BF16)

    qscale = (HEAD_DIM ** -0.5) * LOG2E
    for j in range(db // LANES):
        qp = qn[:, j * LANES:(j + 1) * LANES] * qscale
        kp = kn[:, j * LANES:(j + 1) * LANES]
        qr = pltpu.roll(qp, HEAD_DIM, axis=1)
        kr = pltpu.roll(kp, HEAD_DIM, axis=1)
        qa_ref[:, (2 * j) * LANES:(2 * j + 1) * LANES] = jnp.where(low, qp, 0.0).astype(BF16)
        qa_ref[:, (2 * j + 1) * LANES:(2 * j + 2) * LANES] = jnp.where(low, qr, 0.0).astype(BF16)
        ka_ref[:, (2 * j) * LANES:(2 * j + 1) * LANES] = jnp.where(low, kp, 0.0).astype(BF16)
        ka_ref[:, (2 * j + 1) * LANES:(2 * j + 2) * LANES] = jnp.where(low, kr, 0.0).astype(BF16)

    nh = logf_ref.shape[1]
    z = fl + bf_ref[...]
    lf = jnp.minimum(z, 0.0) - jnp.log1p(jnp.exp(-jnp.abs(z)))
    lf = jnp.where(lane < nh, lf, 0.0)
    logf_ref[...] = lf[:, :nh]

    @pl.when(i % tiles_per_seq == 0)
    def _():
        carry_ref[...] = jnp.zeros_like(carry_ref)

    c = jnp.dot(ltri_ref[...], lf, preferred_element_type=F32, precision=HIGHEST) + carry_ref[...]
    carry_ref[...] = c[c.shape[0] - 1:, :]
    c_ref[...] = c[:, :nh]


def _project(x, g1, wm, wf, bfp, gv, qg, kg, bd, wblk, bias, ltri, n_prompt_tiles, tiles_per_seq):
    nt, d = x.shape
    n_main = wm.shape[1]
    da, db = gv.shape[1], qg.shape[1]
    nh = db // HEAD_DIM
    grid = (nt // TM,)
    row = lambda w: pl.BlockSpec((TM, w), lambda i: (i, 0))
    variant = lambda i: i // n_prompt_tiles
    in_specs = [
        row(d), _full((1, d)), _full((d, n_main)), _full((d, LANES)), _full((1, LANES)),
        _full((1, da)), _full((1, db)), _full((1, db)), _full((db, db)),
        pl.BlockSpec((1, da // HEAD_DIM, TM, TM), lambda i: (variant(i), 0, 0, 0)),
        pl.BlockSpec((1, TM, da), lambda i: (variant(i), 0, 0)),
        _full((TM, TM)),
    ]
    out_shape = [
        jax.ShapeDtypeStruct((nt, da), BF16),
        jax.ShapeDtypeStruct((nt, 2 * db), BF16),
        jax.ShapeDtypeStruct((nt, 2 * db), BF16),
        jax.ShapeDtypeStruct((nt, db), BF16),
        jax.ShapeDtypeStruct((nt, db), F32),
        jax.ShapeDtypeStruct((nt, db), F32),
        jax.ShapeDtypeStruct((nt, nh), F32),
        jax.ShapeDtypeStruct((nt, nh), F32),
        jax.ShapeDtypeStruct((TM, da), F32),
        jax.ShapeDtypeStruct((TM, db), F32),
    ]
    out_specs = [row(da), row(2 * db), row(2 * db), row(db), row(db), row(db), row(nh), row(nh),
                 _full((TM, da)), _full((TM, db))]
    return pl.pallas_call(
        functools.partial(_proj_kernel, tiles_per_seq),
        grid=grid, in_specs=in_specs, out_specs=out_specs, out_shape=out_shape,
        scratch_shapes=[pltpu.VMEM((1, LANES), F32)],
        compiler_params=_cparams(("arbitrary",)), name="proj",
    )(x, g1, wm, wf, bfp, gv, qg, kg, bd, wblk, bias, ltri)


def _flash_kernel(qa_ref, ka_ref, vb_ref, cq_ref, ck_ref, o_ref, m_ref, l_ref, acc_ref):
    qi = pl.program_id(1)
    ki = pl.program_id(2)
    nh = cq_ref.shape[1]

    @pl.when(ki == 0)
    def _():
        m_ref[...] = jnp.full_like(m_ref, NEG_BIG)
        l_ref[...] = jnp.zeros_like(l_ref)
        acc_ref[...] = jnp.zeros_like(acc_ref)

    def step(masked):
        cq = cq_ref[...]
        ck = ck_ref[...]
        if masked:
            row = lax.broadcasted_iota(jnp.int32, (BQ, BK), 0)
            col = lax.broadcasted_iota(jnp.int32, (BQ, BK), 1)
            keep = col <= row
        for hh in range(nh):
            q = qa_ref[:, hh * LANES:(hh + 1) * LANES]
            k = ka_ref[:, hh * LANES:(hh + 1) * LANES]
            s = lax.dot_general(q, k, (((1,), (1,)), ((), ())), preferred_element_type=F32)
            s = s + (cq[:, hh:hh + 1] - ck[hh:hh + 1, :])
            if masked:
                s = jnp.where(keep, s, -jnp.inf)
            m_prev = m_ref[hh]
            m_new = jnp.maximum(m_prev, jnp.max(s, axis=1, keepdims=True))
            alpha = jnp.exp2(m_prev - m_new)
            p = jnp.exp2(s - m_new)
            l_ref[hh] = alpha * l_ref[hh] + jnp.sum(p, axis=1, keepdims=True)
            vpair = vb_ref[:, (hh // 2) * LANES:(hh // 2 + 1) * LANES]
            acc_ref[hh] = alpha * acc_ref[hh] + jnp.dot(p.astype(BF16), vpair,
                                                        preferred_element_type=F32)
            m_ref[hh] = m_new

    @pl.when(ki < qi)
    def _():
        step(False)

    @pl.when(ki == qi)
    def _():
        step(True)
        lane = lax.broadcasted_iota(jnp.int32, (BQ, LANES), 1)
        for j in range(nh // 2):
            o0 = acc_ref[2 * j] / l_ref[2 * j]
            o1 = acc_ref[2 * j + 1] / l_ref[2 * j + 1]
            o_ref[:, j * LANES:(j + 1) * LANES] = jnp.where(lane < HEAD_DIM, o0, o1).astype(o_ref.dtype)


def _flash(qa, ka, vb, c2, c2t, batch, seq):
    db = vb.shape[1]
    nh = db // HEAD_DIM
    nq = seq // BQ
    nk = seq // BK
    qrow = lambda b, qi, ki: (b * nq + qi, 0)
    krow = lambda b, qi, ki: (b * nk + jnp.minimum(ki, qi), 0)
    in_specs = [
        pl.BlockSpec((BQ, 2 * db), qrow),
        pl.BlockSpec((BK, 2 * db), krow),
        pl.BlockSpec((BK, db), krow),
        pl.BlockSpec((BQ, nh), qrow),
        pl.BlockSpec((nh, BK), lambda b, qi, ki: (0, b * nk + jnp.minimum(ki, qi))),
    ]
    return pl.pallas_call(
        _flash_kernel,
        grid=(batch, nq, nk), in_specs=in_specs,
        out_specs=pl.BlockSpec((BQ, db), qrow),
        out_shape=jax.ShapeDtypeStruct((batch * seq, db), BF16),
        scratch_shapes=[pltpu.VMEM((nh, BQ, 1), F32), pltpu.VMEM((nh, BQ, 1), F32),
                        pltpu.VMEM((nh, BQ, LANES), F32)],
        compiler_params=_cparams(("arbitrary", "arbitrary", "arbitrary")), name="fox_prompt",
    )(qa, ka, vb, c2, c2t)


def _decode_kernel(n_pages, pt_ref, qbd_ref, kn_ref, vn_ref, lfn_ref, kc_ref, vc_ref, lfc_ref,
                   ustrict_ref, uincl_ref, o_ref, m_ref, l_ref, acc_ref, carry_ref, cn_ref):
    del pt_ref
    p = pl.program_id(1)
    rows = qbd_ref.shape[1]
    nh = lfc_ref.shape[1]
    s_new = rows // nh

    def expand(t):
        return jnp.concatenate(
            [jnp.broadcast_to(t[hh:hh + 1, :], (s_new, t.shape[1])) for hh in range(nh)], axis=0)

    @pl.when(p == 0)
    def _():
        m_ref[...] = jnp.full_like(m_ref, NEG_BIG)
        l_ref[...] = jnp.zeros_like(l_ref)
        acc_ref[...] = jnp.zeros_like(acc_ref)
        carry_ref[...] = jnp.zeros_like(carry_ref)
        cn_ref[...] = jnp.dot(lfn_ref[0], uincl_ref[...], preferred_element_type=F32,
                              precision=HIGHEST)

    q = qbd_ref[0]
    cnx = expand(cn_ref[...])
    lane = lax.broadcasted_iota(jnp.int32, (rows, PAGE), 1)
    tq = lax.broadcasted_iota(jnp.int32, (rows, PAGE), 0) % s_new
    cn_col = jnp.sum(jnp.where(lane == tq, cnx, 0.0), axis=1, keepdims=True)

    def attend(kb, vbm, bias, keep):
        s = lax.dot_general(q, kb, (((1,), (1,)), ((), ())), preferred_element_type=F32) + bias
        if keep is not None:
            s = jnp.where(keep, s, -jnp.inf)
        m_prev = m_ref[...]
        m_new = jnp.maximum(m_prev, jnp.max(s, axis=1, keepdims=True))
        alpha = jnp.exp(m_prev - m_new)
        pr = jnp.exp(s - m_new)
        l_ref[...] = alpha * l_ref[...] + jnp.sum(pr, axis=1, keepdims=True)
        acc_ref[...] = alpha * acc_ref[...] + jnp.dot(pr.astype(BF16), vbm,
                                                      preferred_element_type=F32)
        m_ref[...] = m_new

    lp = lfc_ref[0]
    d8 = jnp.dot(lp, ustrict_ref[...], preferred_element_type=F32, precision=HIGHEST) + carry_ref[...]
    carry_ref[...] = carry_ref[...] + jnp.sum(lp, axis=1, keepdims=True)
    attend(kc_ref[0].astype(BF16), vc_ref[0].astype(BF16), expand(d8) + cn_col, None)

    @pl.when(p == n_pages - 1)
    def _():
        attend(kn_ref[0], vn_ref[0], cn_col - cnx, lane <= tq)
        out = acc_ref[...] / l_ref[...]
        width = out.shape[1]
        r = lax.broadcasted_iota(jnp.int32, (rows, width), 0) // s_new
        cidx = lax.broadcasted_iota(jnp.int32, (rows, width), 1) // HEAD_DIM
        out = jnp.where(r == cidx, out, 0.0)
        o_ref[0] = jnp.sum(out.reshape(nh, s_new, width), axis=0).astype(o_ref.dtype)


def _decode(pt_flat, qbd, kn_pad, vn_pad, lfn_t, kc, vc, lfc_t, ustrict, uincl, n_pages):
    bd_, rows, db = qbd.shape
    nh = lfc_t.shape[1]
    s_new = rows // nh
    page = lambda b, p, pt: (pt[b * n_pages + (n_pages - 1 - p)], 0, 0)
    seqb = lambda b, p, pt: (b, 0, 0)
    grid_spec = pltpu.PrefetchScalarGridSpec(
        num_scalar_prefetch=1, grid=(bd_, n_pages),
        in_specs=[
            pl.BlockSpec((1, rows, db), seqb),
            pl.BlockSpec((1, PAGE, db), seqb),
            pl.BlockSpec((1, PAGE, db), seqb),
            pl.BlockSpec((1, nh, PAGE), seqb),
            pl.BlockSpec((1, PAGE, db), page),
            pl.BlockSpec((1, PAGE, db), page),
            pl.BlockSpec((1, nh, PAGE), page),
            pl.BlockSpec((PAGE, PAGE), lambda b, p, pt: (0, 0)),
            pl.BlockSpec((PAGE, PAGE), lambda b, p, pt: (0, 0)),
        ],
        out_specs=pl.BlockSpec((1, s_new, db), seqb),
        scratch_shapes=[pltpu.VMEM((rows, 1), F32), pltpu.VMEM((rows, 1), F32),
                        pltpu.VMEM((rows, db), F32), pltpu.VMEM((nh, 1), F32),
                        pltpu.VMEM((nh, PAGE), F32)],
    )
    return pl.pallas_call(
        functools.partial(_decode_kernel, n_pages),
        grid_spec=grid_spec,
        out_shape=jax.ShapeDtypeStruct((bd_, s_new, db), BF16),
        compiler_params=_cparams(("arbitrary", "arbitrary")), name="fox_decode",
    )(pt_flat, qbd, kn_pad, vn_pad, lfn_t, kc, vc, lfc_t, ustrict, uincl)


def _outproj_kernel(n_prompt_tiles, x_ref, oa_ref, obp_ref, obs_ref, woa_ref, wob_ref, g2_ref,
                    wr_ref, br_ref, x1_ref, h2_ref, ti_ref, tg_ref):
    i = pl.program_id(0)
    ob = jnp.where(i >= n_prompt_tiles, obs_ref[...], obp_ref[...])
    mix = (jnp.dot(oa_ref[...], woa_ref[...], preferred_element_type=F32)
           + jnp.dot(ob, wob_ref[...], preferred_element_type=F32))
    x1 = x_ref[...] + mix
    x1_ref[...] = x1
    ms = jnp.mean(x1 * x1, axis=-1, keepdims=True)
    h2 = x1 * lax.rsqrt(ms + EPS) * g2_ref[...]
    h2_ref[...] = h2
    logits = jnp.dot(h2, wr_ref[...], preferred_element_type=F32, precision=HIGHEST) + br_ref[...]
    lane = lax.broadcasted_iota(jnp.int32, logits.shape, 1)
    cur = jnp.where(lane < N_EXPERTS, logits, -jnp.inf)
    idx_out = jnp.zeros(logits.shape, jnp.int32)
    val_out = jnp.zeros(logits.shape, F32)
    v0 = None
    denom = None
    for k in range(TOP_K):
        mval = jnp.max(cur, axis=1, keepdims=True)
        midx = jnp.min(jnp.where(cur == mval, lane, LANES), axis=1, keepdims=True)
        if k == 0:
            v0 = mval
            ek = jnp.ones_like(mval)
            denom = ek
        else:
            ek = jnp.exp(mval - v0)
            denom = denom + ek
        idx_out = jnp.where(lane == k, midx, idx_out)
        val_out = jnp.where(lane == k, ek, val_out)
        cur = jnp.where(lane == midx, -jnp.inf, cur)
    ti_ref[...] = idx_out
    tg_ref[...] = val_out / denom


def _outproj(x, oa, obp, obs, woa, wob, g2, wr, br, n_prompt_tiles):
    nt, d = x.shape
    half = oa.shape[1]
    row = lambda w: pl.BlockSpec((TM, w), lambda i: (i, 0))
    in_specs = [
        row(d), row(half),
        pl.BlockSpec((TM, half), lambda i: (jnp.minimum(i, n_prompt_tiles - 1), 0)),
        _full((TM, half)),
        _full((half, d)), _full((half, d)), _full((1, d)), _full((d, LANES)), _full((1, LANES)),
    ]
    out_shape = [jax.ShapeDtypeStruct((nt, d), F32), jax.ShapeDtypeStruct((nt, d), F32),
                 jax.ShapeDtypeStruct((nt, LANES), jnp.int32), jax.ShapeDtypeStruct((nt, LANES), F32)]
    return pl.pallas_call(
        functools.partial(_outproj_kernel, n_prompt_tiles),
        grid=(nt // TM,), in_specs=in_specs,
        out_specs=[row(d), row(d), row(LANES), row(LANES)], out_shape=out_shape,
        compiler_params=_cparams(("arbitrary",)), name="outproj_router",
    )(x, oa, obp, obs, woa, wob, g2, wr, br)


def _gather_rows(idx_ref, base, n_rows, src_hbm, dst_ref, sem):
    def body(r, carry):
        tok = idx_ref[base + r]
        pltpu.make_async_copy(src_hbm.at[pl.ds(tok, 1)], dst_ref.at[pl.ds(r, 1)], sem).start()
        return carry
    lax.fori_loop(0, n_rows, body, 0, unroll=8)


def _expert_kernel(be_ref, bv_ref, st_ref, h2_hbm, wgu_ref, bgu_ref, wdn_ref, bdn_ref, gate_ref,
                   ys_ref, xs_buf, wgu_bf, wdn_bf, sem):
    i = pl.program_id(0)
    rows = xs_buf.shape[0]
    d_ff = wdn_ref.shape[1]

    @pl.when(bv_ref[i] == 0)
    def _():
        ys_ref[...] = jnp.zeros_like(ys_ref)

    @pl.when(bv_ref[i] == 1)
    def _():
        _gather_rows(st_ref, i * rows, rows, h2_hbm, xs_buf, sem)
        prev = be_ref[jnp.maximum(i - 1, 0)]

        @pl.when((i == 0) | (prev != be_ref[i]))
        def _():
            wgu_bf[...] = wgu_ref[0].astype(BF16)
            wdn_bf[...] = wdn_ref[0].astype(BF16)

        pltpu.make_async_copy(h2_hbm.at[pl.ds(0, rows)], xs_buf, sem).wait()
        xs = xs_buf[...].astype(BF16)
        gu = jnp.dot(xs, wgu_bf[...], preferred_element_type=F32) + bgu_ref[0]
        glu = jnp.minimum(gu[:, :d_ff], SWIGLU_LIMIT)
        lin = jnp.clip(gu[:, d_ff:], -SWIGLU_LIMIT, SWIGLU_LIMIT)
        act = glu * jax.nn.sigmoid(SWIGLU_ALPHA * glu) * (lin + 1.0)
        y = jnp.dot(act.astype(BF16), wdn_bf[...], preferred_element_type=F32) + bdn_ref[0]
        ys_ref[...] = y * gate_ref[...]


def _experts(block_e, block_valid, slot_tok, slot_gate, h2, wgu, bgu, wdn, bdn):
    n_slots = slot_tok.shape[0]
    n_blocks = n_slots // MOE_BLOCK
    d = h2.shape[1]
    ne, _, two_ff = wgu.shape
    d_ff = two_ff // 2
    eidx = lambda i, be, bv, st: (be[i], 0, 0)
    grid_spec = pltpu.PrefetchScalarGridSpec(
        num_scalar_prefetch=3, grid=(n_blocks,),
        in_specs=[
            pl.BlockSpec(memory_space=pl.ANY),
            pl.BlockSpec((1, d, two_ff), eidx),
            pl.BlockSpec((1, 1, two_ff), eidx),
            pl.BlockSpec((1, d_ff, d), eidx),
            pl.BlockSpec((1, 1, d), eidx),
            pl.BlockSpec((MOE_BLOCK, 1), lambda i, be, bv, st: (i, 0)),
        ],
        out_specs=pl.BlockSpec((MOE_BLOCK, d), lambda i, be, bv, st: (i, 0)),
        scratch_shapes=[pltpu.VMEM((MOE_BLOCK, d), F32), pltpu.VMEM((d, two_ff), BF16),
                        pltpu.VMEM((d_ff, d), BF16), pltpu.SemaphoreType.DMA(())],
    )
    return pl.pallas_call(
        _expert_kernel, grid_spec=grid_spec,
        out_shape=jax.ShapeDtypeStruct((n_slots, d), F32),
        compiler_params=_cparams(("arbitrary",)), name="experts",
    )(block_e, block_valid, slot_tok, h2, wgu, bgu.reshape(ne, 1, two_ff), wdn,
      bdn.reshape(ne, 1, d), slot_gate)


def _combine_kernel(dest_ref, x1_ref, ys_hbm, o_ref, buf, sem):
    i = pl.program_id(0)
    rows = x1_ref.shape[0]
    n = TOP_K * rows
    _gather_rows(dest_ref, i * n, n, ys_hbm, buf, sem)
    pltpu.make_async_copy(ys_hbm.at[pl.ds(0, n)], buf, sem).wait()
    acc = x1_ref[...]
    for k in range(TOP_K):
        acc = acc + buf[k * rows:(k + 1) * rows, :]
    o_ref[...] = acc


def _combine(dest_flat, x1, ys):
    nt, d = x1.shape
    grid_spec = pltpu.PrefetchScalarGridSpec(
        num_scalar_prefetch=1, grid=(nt // TM,),
        in_specs=[pl.BlockSpec((TM, d), lambda i, dr: (i, 0)), pl.BlockSpec(memory_space=pl.ANY)],
        out_specs=pl.BlockSpec((TM, d), lambda i, dr: (i, 0)),
        scratch_shapes=[pltpu.VMEM((TOP_K * TM, d), F32), pltpu.SemaphoreType.DMA(())],
    )
    return pl.pallas_call(
        _combine_kernel, grid_spec=grid_spec,
        out_shape=jax.ShapeDtypeStruct((nt, d), F32),
        compiler_params=_cparams(("arbitrary",)), name="combine",
    )(dest_flat, x1, ys)


def _route(top_idx, gates):
    nt = top_idx.shape[0]
    m = nt * TOP_K
    e_flat = top_idx.reshape(m)
    onehot = (e_flat[:, None] == jnp.arange(N_EXPERTS, dtype=jnp.int32)[None, :]).astype(jnp.int32)
    csum = jnp.cumsum(onehot, axis=0)
    sizes = csum[-1]
    pos = jnp.sum(jnp.where(onehot == 1, csum - 1, 0), axis=1)
    psizes = (sizes + MOE_BLOCK - 1) // MOE_BLOCK * MOE_BLOCK
    pend = jnp.cumsum(psizes)
    pstart = pend - psizes
    dest = (pstart[e_flat] + pos).astype(jnp.int32)
    n_blocks = -(-(m + N_EXPERTS * (MOE_BLOCK - 1)) // MOE_BLOCK)
    n_slots = n_blocks * MOE_BLOCK
    tok = jnp.arange(m, dtype=jnp.int32) // TOP_K
    slot_tok = jnp.zeros((n_slots,), jnp.int32).at[dest].set(tok)
    slot_gate = jnp.zeros((n_slots,), F32).at[dest].set(gates.reshape(m)).reshape(n_slots, 1)
    bstart = jnp.arange(n_blocks, dtype=jnp.int32) * MOE_BLOCK
    block_e = jnp.minimum(jnp.searchsorted(pend, bstart, side='right'), N_EXPERTS - 1).astype(jnp.int32)
    block_valid = (bstart < pend[-1]).astype(jnp.int32)
    return dest, slot_tok, slot_gate, block_e, block_valid


def kernel(x_prompt, x_sample, cache_k, cache_v, cache_logf, page_table, norm1_g, w_in, b_f, gv_g,
           w_s, b_s, q_g, k_g, w_out, norm2_g, w_router, b_router, w_gu, b_gu, w_dn, b_dn):
    batch, seq, d = x_prompt.shape
    dec_b, dec_s, _ = x_sample.shape
    depth = w_in.shape[0]
    ha = gv_g.shape[1]
    hb = b_f.shape[1]
    da, db = ha * HEAD_DIM, hb * HEAD_DIM
    n_main = 2 * da + 3 * db
    n_phys = cache_k.shape[1]
    n_pages = page_table.shape[1]
    t_prompt = batch * seq
    t_dec = dec_b * dec_s
    assert t_dec == TM and seq % TM == 0 and TM % CHUNK == 0 and seq % BQ == 0 and BQ == BK
    assert da % LANES == 0 and db % LANES == 0 and dec_s <= CHUNK and TM % dec_s == 0
    n_prompt_tiles = t_prompt // TM
    tiles_per_seq = seq // TM

    x = jnp.concatenate([x_prompt.reshape(t_prompt, d), x_sample.reshape(t_dec, d)], axis=0)

    wm_all = w_in[:, :, :n_main].astype(BF16)
    wf_all = jnp.pad(w_in[:, :, n_main:], ((0, 0), (0, 0), (0, LANES - hb))).astype(BF16)
    bf_all = jnp.pad(b_f, ((0, 0), (0, LANES - hb))).reshape(depth, 1, LANES)
    gv_all = gv_g.reshape(depth, 1, da)
    qg_all = jnp.tile(q_g, (1, hb)).reshape(depth, 1, db)
    kg_all = jnp.tile(k_g, (1, hb)).reshape(depth, 1, db)
    head_of = jnp.arange(db) // HEAD_DIM
    bd = (head_of[:, None] == head_of[None, :]).astype(BF16)
    tril = jnp.tril(jnp.ones((CHUNK, CHUNK), F32))
    wt = w_s * tril
    eye_p = jnp.eye(TM // CHUNK, dtype=F32)
    eye_s = jnp.eye(TM // dec_s, dtype=F32)
    wblk_p = jnp.einsum('ab,lhts->lhatbs', eye_p, wt).reshape(depth, ha, TM, TM)
    wblk_s = jnp.einsum('ab,lhts->lhatbs', eye_s, wt[:, :, :dec_s, :dec_s]).reshape(depth, ha, TM, TM)
    wblk_all = jnp.stack([wblk_p, wblk_s], axis=1).astype(BF16)
    bs_t = jnp.repeat(jnp.swapaxes(b_s, 1, 2), HEAD_DIM, axis=2)
    bias_p = jnp.tile(bs_t, (1, TM // CHUNK, 1))
    bias_s = jnp.tile(bs_t[:, :dec_s], (1, TM // dec_s, 1))
    bias_all = jnp.stack([bias_p, bias_s], axis=1)
    ltri = jnp.tril(jnp.ones((TM, TM), F32))
    woa_all = w_out[:, :da].astype(BF16)
    wob_all = w_out[:, da:].astype(BF16)
    wr_all = jnp.pad(w_router, ((0, 0), (0, 0), (0, LANES - N_EXPERTS)))
    br_all = jnp.pad(b_router, ((0, 0), (0, LANES - N_EXPERTS))).reshape(depth, 1, LANES)

    kc = cache_k.reshape(depth * n_phys, PAGE, db)
    vc = cache_v.reshape(depth * n_phys, PAGE, db)
    lfc_t = jnp.swapaxes(cache_logf, 2, 3).reshape(depth * n_phys, hb, PAGE)
    pt = page_table.reshape(dec_b * n_pages).astype(jnp.int32)
    ar = jnp.arange(PAGE)
    ustrict = (ar[:, None] > ar[None, :]).astype(F32)
    uincl = (ar[:, None] <= ar[None, :]).astype(F32)
    eye_h = jnp.eye(hb, dtype=F32)

    kp_l, vp_l, fp_l, ks_l, vs_l, fs_l, cv_l = [], [], [], [], [], [], []
    for l in range(depth):
        oa, qa, ka, vb, k32, v32, logf, c, va_s, q_s = _project(
            x, norm1_g[l].reshape(1, d), wm_all[l], wf_all[l], bf_all[l], gv_all[l], qg_all[l],
            kg_all[l], bd, wblk_all[l], bias_all[l], ltri, n_prompt_tiles, tiles_per_seq)

        c2 = c[:t_prompt] * LOG2E
        ob_p = _flash(qa, ka, vb, c2, c2.T, batch, seq)

        q4 = q_s.reshape(dec_b, dec_s, hb, HEAD_DIM) * (HEAD_DIM ** -0.5)
        qbd = jnp.einsum('bthd,hg->bhtgd', q4, eye_h).reshape(dec_b, hb * dec_s, db).astype(BF16)
        k_new = k32[t_prompt:].reshape(dec_b, dec_s, db)
        v_new = v32[t_prompt:].reshape(dec_b, dec_s, db)
        lf_new = logf[t_prompt:].reshape(dec_b, dec_s, hb)
        padr = ((0, 0), (0, PAGE - dec_s), (0, 0))
        kn_pad = jnp.pad(k_new, padr).astype(BF16)
        vn_pad = jnp.pad(v_new, padr).astype(BF16)
        lfn_t = jnp.pad(jnp.swapaxes(lf_new, 1, 2), ((0, 0), (0, 0), (0, PAGE - dec_s)))
        ob_s = _decode(pt + l * n_phys, qbd, kn_pad, vn_pad, lfn_t, kc, vc, lfc_t, ustrict, uincl,
                       n_pages).reshape(t_dec, db)

        x1, h2, ti, tg = _outproj(x, oa, ob_p, ob_s, woa_all[l], wob_all[l],
                                  norm2_g[l].reshape(1, d), wr_all[l], br_all[l], n_prompt_tiles)
        dest, slot_tok, slot_gate, block_e, block_valid = _route(ti[:, :TOP_K], tg[:, :TOP_K])
        ys = _experts(block_e, block_valid, slot_tok, slot_gate, h2, w_gu[l], b_gu[l], w_dn[l], b_dn[l])
        nt = x1.shape[0]
        dest_t = dest.reshape(nt // TM, TM, TOP_K).transpose(0, 2, 1).reshape(nt * TOP_K)
        x = _combine(dest_t, x1, ys)

        kp_l.append(k32[:t_prompt].reshape(batch, seq, hb, HEAD_DIM))
        vp_l.append(v32[:t_prompt].reshape(batch, seq, hb, HEAD_DIM))
        fp_l.append(logf[:t_prompt].reshape(batch, seq, hb))
        ks_l.append(k_new.reshape(dec_b, dec_s, hb, HEAD_DIM))
        vs_l.append(v_new.reshape(dec_b, dec_s, hb, HEAD_DIM))
        fs_l.append(lf_new)
        cv_l.append(va_s.reshape(dec_b, dec_s, ha, HEAD_DIM))

    y_prompt = x[:t_prompt].reshape(batch, seq, d)
    y_sample = x[t_prompt:].reshape(dec_b, dec_s, d)
    return (y_prompt, y_sample, jnp.stack(kp_l), jnp.stack(vp_l), jnp.stack(fp_l),
            jnp.stack(ks_l), jnp.stack(vs_l), jnp.stack(fs_l), jnp.stack(cv_l))
```

```python
import functools
import math

import jax
import jax.numpy as jnp
from jax import lax
from jax.experimental import pallas as pl
from jax.experimental.pallas import tpu as pltpu

F32 = jnp.float32
BF16 = jnp.bfloat16
HIGHEST = lax.Precision.HIGHEST

HEAD_DIM = 64
CHUNK = 128
PAGE = 128
N_EXPERTS = 32
TOP_K = 4
EPS = 1e-6
SWIGLU_ALPHA = 1.702
SWIGLU_LIMIT = 7.0
LOG2E = math.log2(math.e)
NEG_BIG = -1e30

LANES = 128
TM = 256
BQ = 512
BK = 512
QC = 128
KC = 256
PAGES_PER_STEP = 8
SUFFIX_ROWS = 4096
MOE_BLOCK = 256
VMEM_LIMIT = 56 * 1024 * 1024

BIAS_LANE = HEAD_DIM
N_SPLIT = 3


def _cparams(sem):
    return pltpu.CompilerParams(dimension_semantics=sem, vmem_limit_bytes=VMEM_LIMIT)


def _full(shape):
    nd = len(shape)
    return pl.BlockSpec(shape, lambda *_: (0,) * nd)


def _proj_kernel(tiles_per_seq,
                 x_ref, g1_ref, wm_ref, wf_ref, bf_ref, gv_ref, qg_ref, kg_ref, bd_ref,
                 wblk_ref, bias_ref, ltri_ref, sel_ref,
                 oa_ref, qa_ref, kt_ref, va_ref, k32_ref, v32_ref, logf_ref,
                 vn_ref, qs_ref, carry_ref):
    i = pl.program_id(0)
    da = gv_ref.shape[1]
    db = qg_ref.shape[1]
    x = x_ref[...]
    ms = jnp.mean(x * x, axis=-1, keepdims=True)
    h = (x * lax.rsqrt(ms + EPS) * g1_ref[...]).astype(BF16)
    p = jnp.dot(h, wm_ref[...], preferred_element_type=F32)
    fl = jnp.dot(h, wf_ref[...], preferred_element_type=F32)

    def group_norm(t, gamma):
        sq = (t * t).astype(BF16)
        gms = jnp.dot(sq, bd_ref[...], preferred_element_type=F32) * (1.0 / HEAD_DIM)
        return t * lax.rsqrt(gms + EPS) * gamma

    u = jax.nn.gelu(p[:, 0:da])
    van = group_norm(jax.nn.gelu(p[:, da:2 * da]), gv_ref[...])
    qn = group_norm(p[:, 2 * da:2 * da + db], qg_ref[...])
    kn = group_norm(p[:, 2 * da + db:2 * da + 2 * db], kg_ref[...])
    vv = p[:, 2 * da + 2 * db:2 * da + 3 * db]

    vn_ref[...] = van
    qs_ref[...] = qn
    k32_ref[...] = kn
    v32_ref[...] = vv

    lane = lax.broadcasted_iota(jnp.int32, (x.shape[0], LANES), 1)
    low = lane < HEAD_DIM

    van_bf = van.astype(BF16)
    for j in range(da // LANES):
        rhs = van_bf[:, j * LANES:(j + 1) * LANES]
        z0 = jnp.dot(wblk_ref[0, 2 * j], rhs, preferred_element_type=F32)
        z1 = jnp.dot(wblk_ref[0, 2 * j + 1], rhs, preferred_element_type=F32)
        z = jnp.where(low, z0, z1) + bias_ref[0, :, j * LANES:(j + 1) * LANES]
        oa_ref[:, j * LANES:(j + 1) * LANES] = (u[:, j * LANES:(j + 1) * LANES] * z).astype(BF16)

    nh = logf_ref.shape[1]
    z = fl + bf_ref[...]
    lf = jnp.minimum(z, 0.0) - jnp.log1p(jnp.exp(-jnp.abs(z)))
    lf = jnp.where(lane < nh, lf, 0.0)
    logf_ref[...] = lf[:, :nh]

    @pl.when(i % tiles_per_seq == 0)
    def _():
        carry_ref[...] = jnp.zeros_like(carry_ref)

    c = jnp.dot(ltri_ref[...], lf, preferred_element_type=F32, precision=HIGHEST) + carry_ref[...]
    carry_ref[...] = c[c.shape[0] - 1:, :]

    c2 = c * LOG2E
    hi = c2.astype(BF16).astype(F32)
    r1 = c2 - hi
    mid = r1.astype(BF16).astype(F32)
    lo = (r1 - mid).astype(BF16).astype(F32)
    c3 = (hi + pltpu.roll(mid, nh, axis=1) + pltpu.roll(lo, 2 * nh, axis=1)
          + jnp.where(lane == N_SPLIT * nh, 1.0, 0.0)).astype(BF16)
    extra = jnp.dot(c3, sel_ref[...], preferred_element_type=F32)

    qscale = (HEAD_DIM ** -0.5) * LOG2E
    kparts = []
    for j in range(db // LANES):
        qp = qn[:, j * LANES:(j + 1) * LANES] * qscale
        kp = kn[:, j * LANES:(j + 1) * LANES]
        vp = vv[:, j * LANES:(j + 1) * LANES]
        for half, (qh, kh, vh) in enumerate(((qp, kp, vp),
                                             (pltpu.roll(qp, HEAD_DIM, axis=1),
                                              pltpu.roll(kp, HEAD_DIM, axis=1),
                                              pltpu.roll(vp, HEAD_DIM, axis=1)))):
            hh = 2 * j + half
            cols = slice(hh * LANES, (hh + 1) * LANES)
            kcols = slice(2 * db + hh * LANES, 2 * db + (hh + 1) * LANES)
            qa_ref[:, cols] = jnp.where(low, qh, extra[:, cols]).astype(BF16)
            va_ref[:, cols] = jnp.where(low, vh, 1.0).astype(BF16)
            kparts.append(jnp.where(low, kh, extra[:, kcols]))
    kt_ref[...] = jnp.concatenate(kparts, axis=1).T.astype(BF16)


def _project(x, g1, wm, wf, bfp, gv, qg, kg, bd, wblk, bias, ltri, sel, n_prompt_tiles, tiles_per_seq):
    nt, d = x.shape
    n_main = wm.shape[1]
    da, db = gv.shape[1], qg.shape[1]
    nh = db // HEAD_DIM
    grid = (nt // TM,)
    row = lambda w: pl.BlockSpec((TM, w), lambda i: (i, 0))
    variant = lambda i: i // n_prompt_tiles
    in_specs = [
        row(d), _full((1, d)), _full((d, n_main)), _full((d, LANES)), _full((1, LANES)),
        _full((1, da)), _full((1, db)), _full((1, db)), _full((db, db)),
        pl.BlockSpec((1, da // HEAD_DIM, TM, TM), lambda i: (variant(i), 0, 0, 0)),
        pl.BlockSpec((1, TM, da), lambda i: (variant(i), 0, 0)),
        _full((TM, TM)), _full((LANES, 4 * db)),
    ]
    out_shape = [
        jax.ShapeDtypeStruct((nt, da), BF16),
        jax.ShapeDtypeStruct((nt, 2 * db), BF16),
        jax.ShapeDtypeStruct((2 * db, nt), BF16),
        jax.ShapeDtypeStruct((nt, 2 * db), BF16),
        jax.ShapeDtypeStruct((nt, db), F32),
        jax.ShapeDtypeStruct((nt, db), F32),
        jax.ShapeDtypeStruct((nt, nh), F32),
        jax.ShapeDtypeStruct((TM, da), F32),
        jax.ShapeDtypeStruct((TM, db), F32),
    ]
    out_specs = [row(da), row(2 * db), pl.BlockSpec((2 * db, TM), lambda i: (0, i)), row(2 * db),
                 row(db), row(db), row(nh), _full((TM, da)), _full((TM, db))]
    return pl.pallas_call(
        functools.partial(_proj_kernel, tiles_per_seq),
        grid=grid, in_specs=in_specs, out_specs=out_specs, out_shape=out_shape,
        scratch_shapes=[pltpu.VMEM((1, LANES), F32)],
        compiler_params=_cparams(("arbitrary",)), name="proj",
    )(x, g1, wm, wf, bfp, gv, qg, kg, bd, wblk, bias, ltri, sel)


def _flash_kernel(qa_ref, kt_ref, va_ref, o_ref, m_ref, acc_ref):
    qi = pl.program_id(1)
    ki = pl.program_id(2)
    nh = va_ref.shape[1] // LANES

    @pl.when(ki == 0)
    def _():
        m_ref[...] = jnp.full_like(m_ref, NEG_BIG)
        acc_ref[...] = jnp.zeros_like(acc_ref)

    def step(diag):
        for hh in range(nh):
            for qc in range(BQ // QC):
                rows = slice(qc * QC, (qc + 1) * QC)
                q = qa_ref[rows, hh * LANES:(hh + 1) * LANES]
                m = m_ref[hh, rows, :]
                acc = acc_ref[hh, rows, :]
                for kc in range(BK // KC):
                    if diag and kc * KC > qc * QC + QC - 1:
                        continue
                    cols = slice(kc * KC, (kc + 1) * KC)
                    s = jnp.dot(q, kt_ref[hh * LANES:(hh + 1) * LANES, cols],
                                preferred_element_type=F32)
                    if diag and kc * KC + KC - 1 > qc * QC:
                        r = lax.broadcasted_iota(jnp.int32, (QC, KC), 0) + qc * QC
                        c = lax.broadcasted_iota(jnp.int32, (QC, KC), 1) + kc * KC
                        s = jnp.where(c <= r, s, -jnp.inf)
                    smax = jnp.maximum(s[:, :LANES], s[:, LANES:])
                    m_new = jnp.maximum(m, jnp.max(smax, axis=1, keepdims=True))
                    alpha = jnp.exp2(m - m_new)
                    pr = jnp.concatenate([jnp.exp2(s[:, :LANES] - m_new),
                                          jnp.exp2(s[:, LANES:] - m_new)], axis=1)
                    acc = alpha * acc + jnp.dot(pr.astype(BF16),
                                                va_ref[cols, hh * LANES:(hh + 1) * LANES],
                                                preferred_element_type=F32)
                    m = m_new
                m_ref[hh, rows, :] = m
                acc_ref[hh, rows, :] = acc

    @pl.when(ki < qi)
    def _():
        step(False)

    @pl.when(ki == qi)
    def _():
        step(True)
        lane = lax.broadcasted_iota(jnp.int32, (BQ, LANES), 1)
        for j in range(nh // 2):
            a0 = acc_ref[2 * j]
            a1 = acc_ref[2 * j + 1]
            num = jnp.where(lane < HEAD_DIM, a0, pltpu.roll(a1, HEAD_DIM, axis=1))
            den = jnp.where(lane < HEAD_DIM, pltpu.roll(a0, HEAD_DIM, axis=1), a1)
            o_ref[:, j * LANES:(j + 1) * LANES] = (num / den).astype(o_ref.dtype)


def _flash(qa, kt, va, batch, seq):
    wide = qa.shape[1]
    db = wide // 2
    nh = db // HEAD_DIM
    nq = seq // BQ
    nk = seq // BK
    qrow = lambda b, qi, ki: (b * nq + qi, 0)
    in_specs = [
        pl.BlockSpec((BQ, wide), qrow),
        pl.BlockSpec((wide, BK), lambda b, qi, ki: (0, b * nk + jnp.minimum(ki, qi))),
        pl.BlockSpec((BK, wide), lambda b, qi, ki: (b * nk + jnp.minimum(ki, qi), 0)),
    ]
    return pl.pallas_call(
        _flash_kernel,
        grid=(batch, nq, nk), in_specs=in_specs,
        out_specs=pl.BlockSpec((BQ, db), qrow),
        out_shape=jax.ShapeDtypeStruct((batch * seq, db), BF16),
        scratch_shapes=[pltpu.VMEM((nh, BQ, LANES), F32), pltpu.VMEM((nh, BQ, LANES), F32)],
        compiler_params=_cparams(("arbitrary", "arbitrary", "arbitrary")), name="fox_prompt",
    )(qa, kt, va)


def _suffix_kernel(lf_ref, u2_ref, o_ref):
    o_ref[...] = jnp.dot(lf_ref[...], u2_ref[...], preferred_element_type=F32, precision=HIGHEST)


def _suffix(lf_rows, u2):
    n = lf_rows.shape[0]
    blk = math.gcd(n, SUFFIX_ROWS)
    return pl.pallas_call(
        _suffix_kernel, grid=(n // blk,),
        in_specs=[pl.BlockSpec((blk, PAGE), lambda i: (i, 0)), _full((PAGE, 2 * PAGE))],
        out_specs=pl.BlockSpec((blk, 2 * PAGE), lambda i: (i, 0)),
        out_shape=jax.ShapeDtypeStruct((n, 2 * PAGE), F32),
        compiler_params=_cparams(("arbitrary",)), name="logf_suffix",
    )(lf_rows, u2)


def _decode_kernel(n_steps, pt_ref, qbd_ref, kn_ref, vn_ref, lfn_ref, uincl_ref, *rest):
    del pt_ref
    g_pages = PAGES_PER_STEP
    kc_refs = rest[0:g_pages]
    vc_refs = rest[g_pages:2 * g_pages]
    ds_refs = rest[2 * g_pages:3 * g_pages]
    o_ref, m_ref, l_ref, acc_ref, carry_ref, cn_ref = rest[3 * g_pages:]
    p = pl.program_id(1)
    rows = qbd_ref.shape[1]
    nh = ds_refs[0].shape[1]
    s_new = rows // nh

    def expand(t):
        return jnp.concatenate(
            [jnp.broadcast_to(t[hh:hh + 1, :], (s_new, t.shape[1])) for hh in range(nh)], axis=0)

    @pl.when(p == 0)
    def _():
        m_ref[...] = jnp.full_like(m_ref, NEG_BIG)
        l_ref[...] = jnp.zeros_like(l_ref)
        acc_ref[...] = jnp.zeros_like(acc_ref)
        carry_ref[...] = jnp.zeros_like(carry_ref)
        cn_ref[...] = jnp.dot(lfn_ref[0], uincl_ref[...], preferred_element_type=F32,
                              precision=HIGHEST)

    q = qbd_ref[0]
    cnx = expand(cn_ref[...])
    lane = lax.broadcasted_iota(jnp.int32, (rows, PAGE), 1)
    tq = lax.broadcasted_iota(jnp.int32, (rows, PAGE), 0) % s_new
    cn_col = jnp.sum(jnp.where(lane == tq, cnx, 0.0), axis=1, keepdims=True)

    def attend(kb, vbm, bias, keep):
        s = lax.dot_general(q, kb, (((1,), (1,)), ((), ())), preferred_element_type=F32) + bias
        if keep is not None:
            s = jnp.where(keep, s, -jnp.inf)
        m_prev = m_ref[...]
        m_new = jnp.maximum(m_prev, jnp.max(s, axis=1, keepdims=True))
        alpha = jnp.exp(m_prev - m_new)
        pr = jnp.exp(s - m_new)
        l_ref[...] = alpha * l_ref[...] + jnp.sum(pr, axis=1, keepdims=True)
        acc_ref[...] = alpha * acc_ref[...] + jnp.dot(pr.astype(BF16), vbm,
                                                      preferred_element_type=F32)
        m_ref[...] = m_new

    def merged(ref):
        parts = [ref[pl.ds(hh, PAGE, stride=nh), :] for hh in range(nh)]
        return jnp.concatenate(parts, axis=1).astype(BF16)

    for g in range(g_pages):
        ds = ds_refs[g][0]
        d8 = ds[:, :PAGE] + carry_ref[...]
        carry_ref[...] = carry_ref[...] + ds[:, PAGE:PAGE + 1]
        attend(merged(kc_refs[g]), merged(vc_refs[g]), expand(d8) + cn_col, None)

    @pl.when(p == n_steps - 1)
    def _():
        attend(kn_ref[0], vn_ref[0], cn_col - cnx, lane <= tq)
        out = acc_ref[...] / l_ref[...]
        width = out.shape[1]
        r = lax.broadcasted_iota(jnp.int32, (rows, width), 0) // s_new
        cidx = lax.broadcasted_iota(jnp.int32, (rows, width), 1) // HEAD_DIM
        out = jnp.where(r == cidx, out, 0.0)
        o_ref[0] = jnp.sum(out.reshape(nh, s_new, width), axis=0).astype(o_ref.dtype)


def _decode(pt_flat, qbd, kn_pad, vn_pad, lfn_t, kc, vc, dsuf, uincl, n_pages):
    bd_, rows, db = qbd.shape
    nh = dsuf.shape[1]
    s_new = rows // nh
    g_pages = PAGES_PER_STEP
    n_steps = n_pages // g_pages
    seqb = lambda b, p, pt: (b, 0, 0)

    def page(g):
        return lambda b, p, pt: (pt[b * n_pages + (n_pages - 1 - (p * g_pages + g))], 0, 0)

    in_specs = [
        pl.BlockSpec((1, rows, db), seqb),
        pl.BlockSpec((1, PAGE, db), seqb),
        pl.BlockSpec((1, PAGE, db), seqb),
        pl.BlockSpec((1, nh, PAGE), seqb),
        pl.BlockSpec((PAGE, PAGE), lambda b, p, pt: (0, 0)),
    ]
    in_specs += [pl.BlockSpec((None, PAGE * nh, HEAD_DIM), page(g)) for g in range(g_pages)]
    in_specs += [pl.BlockSpec((None, PAGE * nh, HEAD_DIM), page(g)) for g in range(g_pages)]
    in_specs += [pl.BlockSpec((1, nh, 2 * PAGE), page(g)) for g in range(g_pages)]
    grid_spec = pltpu.PrefetchScalarGridSpec(
        num_scalar_prefetch=1, grid=(bd_, n_steps), in_specs=in_specs,
        out_specs=pl.BlockSpec((1, s_new, db), seqb),
        scratch_shapes=[pltpu.VMEM((rows, 1), F32), pltpu.VMEM((rows, 1), F32),
                        pltpu.VMEM((rows, db), F32), pltpu.VMEM((nh, 1), F32),
                        pltpu.VMEM((nh, PAGE), F32)],
    )
    return pl.pallas_call(
        functools.partial(_decode_kernel, n_steps), grid_spec=grid_spec,
        out_shape=jax.ShapeDtypeStruct((bd_, s_new, db), BF16),
        compiler_params=_cparams(("arbitrary", "arbitrary")), name="fox_decode",
    )(pt_flat, qbd, kn_pad, vn_pad, lfn_t, uincl, *([kc] * g_pages), *([vc] * g_pages),
      *([dsuf] * g_pages))


def _outproj_kernel(n_prompt_tiles, x_ref, oa_ref, obp_ref, obs_ref, woa_ref, wob_ref, g2_ref,
                    wr_ref, br_ref, x1_ref, h2_ref, ti_ref, tg_ref):
    i = pl.program_id(0)
    ob = jnp.where(i >= n_prompt_tiles, obs_ref[...], obp_ref[...])
    mix = (jnp.dot(oa_ref[...], woa_ref[...], preferred_element_type=F32)
           + jnp.dot(ob, wob_ref[...], preferred_element_type=F32))
    x1 = x_ref[...] + mix
    x1_ref[...] = x1
    ms = jnp.mean(x1 * x1, axis=-1, keepdims=True)
    h2 = x1 * lax.rsqrt(ms + EPS) * g2_ref[...]
    h2_ref[...] = h2
    logits = jnp.dot(h2, wr_ref[...], preferred_element_type=F32, precision=HIGHEST) + br_ref[...]
    lane = lax.broadcasted_iota(jnp.int32, logits.shape, 1)
    cur = jnp.where(lane < N_EXPERTS, logits, -jnp.inf)
    idx_out = jnp.zeros(logits.shape, jnp.int32)
    val_out = jnp.zeros(logits.shape, F32)
    v0 = None
    denom = None
    for k in range(TOP_K):
        mval = jnp.max(cur, axis=1, keepdims=True)
        midx = jnp.min(jnp.where(cur == mval, lane, LANES), axis=1, keepdims=True)
        if k == 0:
            v0 = mval
            ek = jnp.ones_like(mval)
            denom = ek
        else:
            ek = jnp.exp(mval - v0)
            denom = denom + ek
        idx_out = jnp.where(lane == k, midx, idx_out)
        val_out = jnp.where(lane == k, ek, val_out)
        cur = jnp.where(lane == midx, -jnp.inf, cur)
    ti_ref[...] = idx_out
    tg_ref[...] = val_out / denom


def _outproj(x, oa, obp, obs, woa, wob, g2, wr, br, n_prompt_tiles):
    nt, d = x.shape
    half = oa.shape[1]
    row = lambda w: pl.BlockSpec((TM, w), lambda i: (i, 0))
    in_specs = [
        row(d), row(half),
        pl.BlockSpec((TM, half), lambda i: (jnp.minimum(i, n_prompt_tiles - 1), 0)),
        _full((TM, half)),
        _full((half, d)), _full((half, d)), _full((1, d)), _full((d, LANES)), _full((1, LANES)),
    ]
    out_shape = [jax.ShapeDtypeStruct((nt, d), F32), jax.ShapeDtypeStruct((nt, d), F32),
                 jax.ShapeDtypeStruct((nt, LANES), jnp.int32), jax.ShapeDtypeStruct((nt, LANES), F32)]
    return pl.pallas_call(
        functools.partial(_outproj_kernel, n_prompt_tiles),
        grid=(nt // TM,), in_specs=in_specs,
        out_specs=[row(d), row(d), row(LANES), row(LANES)], out_shape=out_shape,
        compiler_params=_cparams(("arbitrary",)), name="outproj_router",
    )(x, oa, obp, obs, woa, wob, g2, wr, br)


def _start_row_gather(idx_ref, base, n_rows, src_hbm, dst_ref, sem):
    def body(r, carry):
        tok = idx_ref[base + r]
        pltpu.make_async_copy(src_hbm.at[pl.ds(tok, 1)], dst_ref.at[pl.ds(r, 1)], sem).start()
        return carry
    lax.fori_loop(0, n_rows, body, 0, unroll=8)


def _wait_row_gather(n_rows, src_hbm, dst_ref, sem):
    pltpu.make_async_copy(src_hbm.at[pl.ds(0, n_rows)], dst_ref, sem).wait()


def _expert_kernel(be_ref, bv_ref, st_ref, h2_hbm, wgu_ref, bgu_ref, wdn_ref, bdn_ref,
                   ys_ref, xs_buf, wgu_bf, wdn_bf, sems):
    i = pl.program_id(0)
    n_blocks = pl.num_programs(0)
    rows = xs_buf.shape[1]
    d_ff = wdn_ref.shape[1]
    slot = i % 2

    @pl.when(bv_ref[i] == 0)
    def _():
        ys_ref[...] = jnp.zeros_like(ys_ref)

    @pl.when((i == 0) & (bv_ref[0] == 1))
    def _():
        _start_row_gather(st_ref, 0, rows, h2_hbm, xs_buf.at[0], sems.at[0])

    @pl.when(bv_ref[i] == 1)
    def _():
        nxt = jnp.minimum(i + 1, n_blocks - 1)

        @pl.when((i + 1 < n_blocks) & (bv_ref[nxt] == 1))
        def _():
            _start_row_gather(st_ref, nxt * rows, rows, h2_hbm, xs_buf.at[1 - slot],
                              sems.at[1 - slot])

        prev = be_ref[jnp.maximum(i - 1, 0)]

        @pl.when((i == 0) | (prev != be_ref[i]))
        def _():
            wgu_bf[...] = wgu_ref[0].astype(BF16)
            wdn_bf[...] = wdn_ref[0].astype(BF16)

        _wait_row_gather(rows, h2_hbm, xs_buf.at[slot], sems.at[slot])
        xs = xs_buf[slot].astype(BF16)
        gu = jnp.dot(xs, wgu_bf[...], preferred_element_type=F32) + bgu_ref[0]
        glu = jnp.minimum(gu[:, :d_ff], SWIGLU_LIMIT)
        lin = jnp.clip(gu[:, d_ff:], -SWIGLU_LIMIT, SWIGLU_LIMIT)
        act = glu * jax.nn.sigmoid(SWIGLU_ALPHA * glu) * (lin + 1.0)
        ys_ref[...] = jnp.dot(act.astype(BF16), wdn_bf[...], preferred_element_type=F32) + bdn_ref[0]


def _experts(block_e, block_valid, slot_tok, h2, wgu, bgu, wdn, bdn):
    n_slots = slot_tok.shape[0]
    n_blocks = n_slots // MOE_BLOCK
    d = h2.shape[1]
    ne, _, two_ff = wgu.shape
    d_ff = two_ff // 2
    eidx = lambda i, be, bv, st: (be[i], 0, 0)
    grid_spec = pltpu.PrefetchScalarGridSpec(
        num_scalar_prefetch=3, grid=(n_blocks,),
        in_specs=[
            pl.BlockSpec(memory_space=pl.ANY),
            pl.BlockSpec((1, d, two_ff), eidx),
            pl.BlockSpec((1, 1, two_ff), eidx),
            pl.BlockSpec((1, d_ff, d), eidx),
            pl.BlockSpec((1, 1, d), eidx),
        ],
        out_specs=pl.BlockSpec((MOE_BLOCK, d), lambda i, be, bv, st: (i, 0)),
        scratch_shapes=[pltpu.VMEM((2, MOE_BLOCK, d), F32), pltpu.VMEM((d, two_ff), BF16),
                        pltpu.VMEM((d_ff, d), BF16), pltpu.SemaphoreType.DMA((2,))],
    )
    return pl.pallas_call(
        _expert_kernel, grid_spec=grid_spec,
        out_shape=jax.ShapeDtypeStruct((n_slots, d), F32),
        compiler_params=_cparams(("arbitrary",)), name="experts",
    )(block_e, block_valid, slot_tok, h2, wgu, bgu.reshape(ne, 1, two_ff), wdn,
      bdn.reshape(ne, 1, d))


def _combine_kernel(dest_ref, x1_ref, tg_ref, ys_hbm, o_ref, buf, sems):
    i = pl.program_id(0)
    n_tiles = pl.num_programs(0)
    rows = x1_ref.shape[0]
    n = TOP_K * rows
    slot = i % 2

    @pl.when(i == 0)
    def _():
        _start_row_gather(dest_ref, 0, n, ys_hbm, buf.at[0], sems.at[0])

    @pl.when(i + 1 < n_tiles)
    def _():
        _start_row_gather(dest_ref, (i + 1) * n, n, ys_hbm, buf.at[1 - slot], sems.at[1 - slot])

    _wait_row_gather(n, ys_hbm, buf.at[slot], sems.at[slot])
    acc = x1_ref[...]
    gates = tg_ref[...]
    for k in range(TOP_K):
        acc = acc + buf[slot, k * rows:(k + 1) * rows, :] * gates[:, k:k + 1]
    o_ref[...] = acc


def _combine(dest_flat, x1, tg, ys):
    nt, d = x1.shape
    grid_spec = pltpu.PrefetchScalarGridSpec(
        num_scalar_prefetch=1, grid=(nt // TM,),
        in_specs=[pl.BlockSpec((TM, d), lambda i, dr: (i, 0)),
                  pl.BlockSpec((TM, LANES), lambda i, dr: (i, 0)),
                  pl.BlockSpec(memory_space=pl.ANY)],
        out_specs=pl.BlockSpec((TM, d), lambda i, dr: (i, 0)),
        scratch_shapes=[pltpu.VMEM((2, TOP_K * TM, d), F32), pltpu.SemaphoreType.DMA((2,))],
    )
    return pl.pallas_call(
        _combine_kernel, grid_spec=grid_spec,
        out_shape=jax.ShapeDtypeStruct((nt, d), F32),
        compiler_params=_cparams(("arbitrary",)), name="combine",
    )(dest_flat, x1, tg, ys)


def _route(top_idx):
    nt = top_idx.shape[0]
    m = nt * TOP_K
    e_flat = top_idx.reshape(m)
    onehot = (e_flat[:, None] == jnp.arange(N_EXPERTS, dtype=jnp.int32)[None, :]).astype(jnp.int32)
    csum = jnp.cumsum(onehot, axis=0)
    sizes = csum[-1]
    pos = jnp.sum(jnp.where(onehot == 1, csum - 1, 0), axis=1)
    psizes = (sizes + MOE_BLOCK - 1) // MOE_BLOCK * MOE_BLOCK
    pend = jnp.cumsum(psizes)
    pstart = pend - psizes
    dest = (jnp.sum(onehot * pstart[None, :], axis=1) + pos).astype(jnp.int32)
    n_blocks = -(-(m + N_EXPERTS * (MOE_BLOCK - 1)) // MOE_BLOCK)
    n_slots = n_blocks * MOE_BLOCK
    tok = jnp.arange(m, dtype=jnp.int32) // TOP_K
    slot_tok = jnp.zeros((n_slots,), jnp.int32).at[dest].set(tok)
    bstart = jnp.arange(n_blocks, dtype=jnp.int32) * MOE_BLOCK
    block_e = jnp.minimum(jnp.sum((pend[None, :] <= bstart[:, None]).astype(jnp.int32), axis=1),
                          N_EXPERTS - 1).astype(jnp.int32)
    block_valid = (bstart < pend[-1]).astype(jnp.int32)
    return dest, slot_tok, block_e, block_valid


def _bias_selector(nh, db):
    rows = jnp.arange(LANES)[:, None]
    cols = jnp.arange(2 * db)[None, :]
    head = cols // LANES
    off = cols % LANES - BIAS_LANE
    part = rows // nh
    is_split = (part < N_SPLIT) & (rows % nh == head)
    is_one = rows == N_SPLIT * nh
    q_sel = ((is_split & (off == part)) | (is_one & (off >= N_SPLIT) & (off < 2 * N_SPLIT)))
    k_sel = (is_one & (off >= 0) & (off < N_SPLIT)).astype(F32) \
        - (is_split & (off == part + N_SPLIT)).astype(F32)
    return jnp.concatenate([q_sel.astype(F32), k_sel], axis=1).astype(BF16)


def kernel(x_prompt, x_sample, cache_k, cache_v, cache_logf, page_table, norm1_g, w_in, b_f, gv_g,
           w_s, b_s, q_g, k_g, w_out, norm2_g, w_router, b_router, w_gu, b_gu, w_dn, b_dn):
    batch, seq, d = x_prompt.shape
    dec_b, dec_s, _ = x_sample.shape
    depth = w_in.shape[0]
    ha = gv_g.shape[1]
    hb = b_f.shape[1]
    da, db = ha * HEAD_DIM, hb * HEAD_DIM
    n_main = 2 * da + 3 * db
    n_phys = cache_k.shape[1]
    n_pages = page_table.shape[1]
    t_prompt = batch * seq
    t_dec = dec_b * dec_s
    assert t_dec == TM and seq % TM == 0 and TM % CHUNK == 0 and seq % BQ == 0 and BQ == BK
    assert da % LANES == 0 and db % LANES == 0 and dec_s <= CHUNK and TM % dec_s == 0
    assert cache_k.shape[2] == PAGE and n_pages % PAGES_PER_STEP == 0
    assert (N_SPLIT + 1) * hb <= LANES
    n_prompt_tiles = t_prompt // TM
    tiles_per_seq = seq // TM

    x = jnp.concatenate([x_prompt.reshape(t_prompt, d), x_sample.reshape(t_dec, d)], axis=0)

    wm_all = w_in[:, :, :n_main].astype(BF16)
    wf_all = jnp.pad(w_in[:, :, n_main:], ((0, 0), (0, 0), (0, LANES - hb))).astype(BF16)
    bf_all = jnp.pad(b_f, ((0, 0), (0, LANES - hb))).reshape(depth, 1, LANES)
    gv_all = gv_g.reshape(depth, 1, da)
    qg_all = jnp.tile(q_g, (1, hb)).reshape(depth, 1, db)
    kg_all = jnp.tile(k_g, (1, hb)).reshape(depth, 1, db)
    head_of = jnp.arange(db) // HEAD_DIM
    bd = (head_of[:, None] == head_of[None, :]).astype(BF16)
    tril = jnp.tril(jnp.ones((CHUNK, CHUNK), F32))
    wt = w_s * tril
    eye_p = jnp.eye(TM // CHUNK, dtype=F32)
    eye_s = jnp.eye(TM // dec_s, dtype=F32)
    wblk_p = jnp.einsum('ab,lhts->lhatbs', eye_p, wt).reshape(depth, ha, TM, TM)
    wblk_s = jnp.einsum('ab,lhts->lhatbs', eye_s, wt[:, :, :dec_s, :dec_s]).reshape(depth, ha, TM, TM)
    wblk_all = jnp.stack([wblk_p, wblk_s], axis=1).astype(BF16)
    bs_t = jnp.repeat(jnp.swapaxes(b_s, 1, 2), HEAD_DIM, axis=2)
    bias_p = jnp.tile(bs_t, (1, TM // CHUNK, 1))
    bias_s = jnp.tile(bs_t[:, :dec_s], (1, TM // dec_s, 1))
    bias_all = jnp.stack([bias_p, bias_s], axis=1)
    ltri = jnp.tril(jnp.ones((TM, TM), F32))
    sel = _bias_selector(hb, db)
    woa_all = w_out[:, :da].astype(BF16)
    wob_all = w_out[:, da:].astype(BF16)
    wr_all = jnp.pad(w_router, ((0, 0), (0, 0), (0, LANES - N_EXPERTS)))
    br_all = jnp.pad(b_router, ((0, 0), (0, LANES - N_EXPERTS))).reshape(depth, 1, LANES)

    kc = cache_k.reshape(depth * n_phys, PAGE * hb, HEAD_DIM)
    vc = cache_v.reshape(depth * n_phys, PAGE * hb, HEAD_DIM)
    lfc_rows = jnp.swapaxes(cache_logf, 2, 3).reshape(depth * n_phys * hb, PAGE)
    pt = page_table.reshape(dec_b * n_pages).astype(jnp.int32)
    ar = jnp.arange(PAGE)
    ustrict = (ar[:, None] > ar[None, :]).astype(F32)
    uincl = (ar[:, None] <= ar[None, :]).astype(F32)
    u2 = jnp.concatenate([ustrict, jnp.ones((PAGE, PAGE), F32)], axis=1)
    dsuf = _suffix(lfc_rows, u2).reshape(depth * n_phys, hb, 2 * PAGE)
    eye_h = jnp.eye(hb, dtype=F32)

    kp_l, vp_l, fp_l, ks_l, vs_l, fs_l, cv_l = [], [], [], [], [], [], []
    for l in range(depth):
        oa, qa, kt, va, k32, v32, logf, va_s, q_s = _project(
            x, norm1_g[l].reshape(1, d), wm_all[l], wf_all[l], bf_all[l], gv_all[l], qg_all[l],
            kg_all[l], bd, wblk_all[l], bias_all[l], ltri, sel, n_prompt_tiles, tiles_per_seq)

        ob_p = _flash(qa, kt, va, batch, seq)

        q4 = q_s.reshape(dec_b, dec_s, hb, HEAD_DIM) * (HEAD_DIM ** -0.5)
        qbd = jnp.einsum('bthd,hg->bhtgd', q4, eye_h).reshape(dec_b, hb * dec_s, db).astype(BF16)
        k_new = k32[t_prompt:].reshape(dec_b, dec_s, db)
        v_new = v32[t_prompt:].reshape(dec_b, dec_s, db)
        lf_new = logf[t_prompt:].reshape(dec_b, dec_s, hb)
        padr = ((0, 0), (0, PAGE - dec_s), (0, 0))
        kn_pad = jnp.pad(k_new, padr).astype(BF16)
        vn_pad = jnp.pad(v_new, padr).astype(BF16)
        lfn_t = jnp.pad(jnp.swapaxes(lf_new, 1, 2), ((0, 0), (0, 0), (0, PAGE - dec_s)))
        ob_s = _decode(pt + l * n_phys, qbd, kn_pad, vn_pad, lfn_t, kc, vc, dsuf, uincl,
                       n_pages).reshape(t_dec, db)

        x1, h2, ti, tg = _outproj(x, oa, ob_p, ob_s, woa_all[l], wob_all[l],
                                  norm2_g[l].reshape(1, d), wr_all[l], br_all[l], n_prompt_tiles)
        dest, slot_tok, block_e, block_valid = _route(ti[:, :TOP_K])
        ys = _experts(block_e, block_valid, slot_tok, h2, w_gu[l], b_gu[l], w_dn[l], b_dn[l])
        nt = x1.shape[0]
        dest_t = dest.reshape(nt // TM, TM, TOP_K).transpose(0, 2, 1).reshape(nt * TOP_K)
        x = _combine(dest_t, x1, tg, ys)

        kp_l.append(k32[:t_prompt].reshape(batch, seq, hb, HEAD_DIM))
        vp_l.append(v32[:t_prompt].reshape(batch, seq, hb, HEAD_DIM))
        fp_l.append(logf[:t_prompt].reshape(batch, seq, hb))
        ks_l.append(k_new.reshape(dec_b, dec_s, hb, HEAD_DIM))
        vs_l.append(v_new.reshape(dec_b, dec_s, hb, HEAD_DIM))
        fs_l.append(lf_new)
        cv_l.append(va_s.reshape(dec_b, dec_s, ha, HEAD_DIM))

    y_prompt = x[:t_prompt].reshape(batch, seq, d)
    y_sample = x[t_prompt:].reshape(dec_b, dec_s, d)
    return (y_prompt, y_sample, jnp.stack(kp_l), jnp.stack(vp_l), jnp.stack(fp_l),
            jnp.stack(ks_l), jnp.stack(vs_l), jnp.stack(fs_l), jnp.stack(cv_l))
```

```python
import functools
import math

import jax
import jax.numpy as jnp
from jax import lax
from jax.experimental import pallas as pl
from jax.experimental.pallas import tpu as pltpu

F32 = jnp.float32
BF16 = jnp.bfloat16
HIGHEST = lax.Precision.HIGHEST

HEAD_DIM = 64
CHUNK = 128
PAGE = 128
N_EXPERTS = 32
TOP_K = 4
EPS = 1e-6
SWIGLU_ALPHA = 1.702
SWIGLU_LIMIT = 7.0
LOG2E = math.log2(math.e)
NEG_BIG = -1e30

LANES = 128
TM = 256
BQ = 512
BK = 512
QC = 128
KC = 256
PAGES_PER_STEP = 8
SUFFIX_ROWS = 4096
MOE_BLOCK = 256
VMEM_LIMIT = 56 * 1024 * 1024

BIAS_LANE = HEAD_DIM
N_SPLIT = 3


def _cparams(sem):
    return pltpu.CompilerParams(dimension_semantics=sem, vmem_limit_bytes=VMEM_LIMIT)


def _full(shape):
    nd = len(shape)
    return pl.BlockSpec(shape, lambda *_: (0,) * nd)


def _proj_kernel(tiles_per_seq,
                 x_ref, g1_ref, wm_ref, wf_ref, bf_ref, gv_ref, qg_ref, kg_ref, bd_ref,
                 wblk_ref, bias_ref, ltri_ref, sel_ref,
                 oa_ref, qa_ref, kt_ref, va_ref, k32_ref, v32_ref, logf_ref,
                 vn_ref, qs_ref, carry_ref):
    i = pl.program_id(0)
    da = gv_ref.shape[1]
    db = qg_ref.shape[1]
    x = x_ref[...]
    ms = jnp.mean(x * x, axis=-1, keepdims=True)
    h = (x * lax.rsqrt(ms + EPS) * g1_ref[...]).astype(BF16)
    p = jnp.dot(h, wm_ref[...], preferred_element_type=F32)
    fl = jnp.dot(h, wf_ref[...], preferred_element_type=F32)

    def group_norm(t, gamma):
        sq = (t * t).astype(BF16)
        gms = jnp.dot(sq, bd_ref[...], preferred_element_type=F32) * (1.0 / HEAD_DIM)
        return t * lax.rsqrt(gms + EPS) * gamma

    u = jax.nn.gelu(p[:, 0:da])
    van = group_norm(jax.nn.gelu(p[:, da:2 * da]), gv_ref[...])
    qn = group_norm(p[:, 2 * da:2 * da + db], qg_ref[...])
    kn = group_norm(p[:, 2 * da + db:2 * da + 2 * db], kg_ref[...])
    vv = p[:, 2 * da + 2 * db:2 * da + 3 * db]

    vn_ref[...] = van
    qs_ref[...] = qn
    k32_ref[...] = kn
    v32_ref[...] = vv

    lane = lax.broadcasted_iota(jnp.int32, (x.shape[0], LANES), 1)
    low = lane < HEAD_DIM

    van_bf = van.astype(BF16)
    for j in range(da // LANES):
        rhs = van_bf[:, j * LANES:(j + 1) * LANES]
        z0 = jnp.dot(wblk_ref[0, 2 * j], rhs, preferred_element_type=F32)
        z1 = jnp.dot(wblk_ref[0, 2 * j + 1], rhs, preferred_element_type=F32)
        z = jnp.where(low, z0, z1) + bias_ref[0, :, j * LANES:(j + 1) * LANES]
        oa_ref[:, j * LANES:(j + 1) * LANES] = (u[:, j * LANES:(j + 1) * LANES] * z).astype(BF16)

    nh = logf_ref.shape[1]
    z = fl + bf_ref[...]
    lf = jnp.minimum(z, 0.0) - jnp.log1p(jnp.exp(-jnp.abs(z)))
    lf = jnp.where(lane < nh, lf, 0.0)
    logf_ref[...] = lf[:, :nh]

    @pl.when(i % tiles_per_seq == 0)
    def _():
        carry_ref[...] = jnp.zeros_like(carry_ref)

    c = jnp.dot(ltri_ref[...], lf, preferred_element_type=F32, precision=HIGHEST) + carry_ref[...]
    carry_ref[...] = c[c.shape[0] - 1:, :]

    c2 = c * LOG2E
    hi = c2.astype(BF16).astype(F32)
    r1 = c2 - hi
    mid = r1.astype(BF16).astype(F32)
    lo = (r1 - mid).astype(BF16).astype(F32)
    c3 = (hi + pltpu.roll(mid, nh, axis=1) + pltpu.roll(lo, 2 * nh, axis=1)
          + jnp.where(lane == N_SPLIT * nh, 1.0, 0.0)).astype(BF16)
    extra = jnp.dot(c3, sel_ref[...], preferred_element_type=F32)

    qscale = (HEAD_DIM ** -0.5) * LOG2E
    kparts = []
    for j in range(db // LANES):
        qp = qn[:, j * LANES:(j + 1) * LANES] * qscale
        kp = kn[:, j * LANES:(j + 1) * LANES]
        vp = vv[:, j * LANES:(j + 1) * LANES]
        for half, (qh, kh, vh) in enumerate(((qp, kp, vp),
                                             (pltpu.roll(qp, HEAD_DIM, axis=1),
                                              pltpu.roll(kp, HEAD_DIM, axis=1),
                                              pltpu.roll(vp, HEAD_DIM, axis=1)))):
            hh = 2 * j + half
            cols = slice(hh * LANES, (hh + 1) * LANES)
            kcols = slice(2 * db + hh * LANES, 2 * db + (hh + 1) * LANES)
            qa_ref[:, cols] = jnp.where(low, qh, extra[:, cols]).astype(BF16)
            va_ref[:, cols] = jnp.where(low, vh, 1.0).astype(BF16)
            kparts.append(jnp.where(low, kh, extra[:, kcols]))
    kt_ref[...] = jnp.concatenate(kparts, axis=1).T.astype(BF16)


def _project(x, g1, wm, wf, bfp, gv, qg, kg, bd, wblk, bias, ltri, sel, n_prompt_tiles, tiles_per_seq):
    nt, d = x.shape
    n_main = wm.shape[1]
    da, db = gv.shape[1], qg.shape[1]
    nh = db // HEAD_DIM
    grid = (nt // TM,)
    row = lambda w: pl.BlockSpec((TM, w), lambda i: (i, 0))
    variant = lambda i: i // n_prompt_tiles
    in_specs = [
        row(d), _full((1, d)), _full((d, n_main)), _full((d, LANES)), _full((1, LANES)),
        _full((1, da)), _full((1, db)), _full((1, db)), _full((db, db)),
        pl.BlockSpec((1, da // HEAD_DIM, TM, TM), lambda i: (variant(i), 0, 0, 0)),
        pl.BlockSpec((1, TM, da), lambda i: (variant(i), 0, 0)),
        _full((TM, TM)), _full((LANES, 4 * db)),
    ]
    out_shape = [
        jax.ShapeDtypeStruct((nt, da), BF16),
        jax.ShapeDtypeStruct((nt, 2 * db), BF16),
        jax.ShapeDtypeStruct((2 * db, nt), BF16),
        jax.ShapeDtypeStruct((nt, 2 * db), BF16),
        jax.ShapeDtypeStruct((nt, db), F32),
        jax.ShapeDtypeStruct((nt, db), F32),
        jax.ShapeDtypeStruct((nt, nh), F32),
        jax.ShapeDtypeStruct((TM, da), F32),
        jax.ShapeDtypeStruct((TM, db), F32),
    ]
    out_specs = [row(da), row(2 * db), pl.BlockSpec((2 * db, TM), lambda i: (0, i)), row(2 * db),
                 row(db), row(db), row(nh), _full((TM, da)), _full((TM, db))]
    return pl.pallas_call(
        functools.partial(_proj_kernel, tiles_per_seq),
        grid=grid, in_specs=in_specs, out_specs=out_specs, out_shape=out_shape,
        scratch_shapes=[pltpu.VMEM((1, LANES), F32)],
        compiler_params=_cparams(("arbitrary",)), name="proj",
    )(x, g1, wm, wf, bfp, gv, qg, kg, bd, wblk, bias, ltri, sel)


def _flash_kernel(qa_ref, kt_ref, va_ref, o_ref, m_ref, acc_ref):
    qi = pl.program_id(1)
    ki = pl.program_id(2)
    nh = va_ref.shape[1] // LANES

    @pl.when(ki == 0)
    def _():
        m_ref[...] = jnp.full_like(m_ref, NEG_BIG)
        acc_ref[...] = jnp.zeros_like(acc_ref)

    def step(diag):
        for hh in range(nh):
            for qc in range(BQ // QC):
                rows = slice(qc * QC, (qc + 1) * QC)
                q = qa_ref[rows, hh * LANES:(hh + 1) * LANES]
                m = m_ref[hh, rows, :]
                acc = acc_ref[hh, rows, :]
                for kc in range(BK // KC):
                    if diag and kc * KC > qc * QC + QC - 1:
                        continue
                    cols = slice(kc * KC, (kc + 1) * KC)
                    s = jnp.dot(q, kt_ref[hh * LANES:(hh + 1) * LANES, cols],
                                preferred_element_type=F32)
                    if diag and kc * KC + KC - 1 > qc * QC:
                        r = lax.broadcasted_iota(jnp.int32, (QC, KC), 0) + qc * QC
                        c = lax.broadcasted_iota(jnp.int32, (QC, KC), 1) + kc * KC
                        s = jnp.where(c <= r, s, -jnp.inf)
                    smax = jnp.maximum(s[:, :LANES], s[:, LANES:])
                    m_new = jnp.maximum(m, jnp.max(smax, axis=1, keepdims=True))
                    alpha = jnp.exp2(m - m_new)
                    pr = jnp.concatenate([jnp.exp2(s[:, :LANES] - m_new),
                                          jnp.exp2(s[:, LANES:] - m_new)], axis=1)
                    acc = alpha * acc + jnp.dot(pr.astype(BF16),
                                                va_ref[cols, hh * LANES:(hh + 1) * LANES],
                                                preferred_element_type=F32)
                    m = m_new
                m_ref[hh, rows, :] = m
                acc_ref[hh, rows, :] = acc

    @pl.when(ki < qi)
    def _():
        step(False)

    @pl.when(ki == qi)
    def _():
        step(True)
        lane = lax.broadcasted_iota(jnp.int32, (BQ, LANES), 1)
        for j in range(nh // 2):
            a0 = acc_ref[2 * j]
            a1 = acc_ref[2 * j + 1]
            num = jnp.where(lane < HEAD_DIM, a0, pltpu.roll(a1, HEAD_DIM, axis=1))
            den = jnp.where(lane < HEAD_DIM, pltpu.roll(a0, HEAD_DIM, axis=1), a1)
            o_ref[:, j * LANES:(j + 1) * LANES] = (num / den).astype(o_ref.dtype)


def _flash(qa, kt, va, batch, seq):
    wide = qa.shape[1]
    db = wide // 2
    nh = db // HEAD_DIM
    nq = seq // BQ
    nk = seq // BK
    qrow = lambda b, qi, ki: (b * nq + qi, 0)
    in_specs = [
        pl.BlockSpec((BQ, wide), qrow),
        pl.BlockSpec((wide, BK), lambda b, qi, ki: (0, b * nk + jnp.minimum(ki, qi))),
        pl.BlockSpec((BK, wide), lambda b, qi, ki: (b * nk + jnp.minimum(ki, qi), 0)),
    ]
    return pl.pallas_call(
        _flash_kernel,
        grid=(batch, nq, nk), in_specs=in_specs,
        out_specs=pl.BlockSpec((BQ, db), qrow),
        out_shape=jax.ShapeDtypeStruct((batch * seq, db), BF16),
        scratch_shapes=[pltpu.VMEM((nh, BQ, LANES), F32), pltpu.VMEM((nh, BQ, LANES), F32)],
        compiler_params=_cparams(("arbitrary", "arbitrary", "arbitrary")), name="fox_prompt",
    )(qa, kt, va)


def _suffix_kernel(lf_ref, u2_ref, o_ref):
    o_ref[...] = jnp.dot(lf_ref[...], u2_ref[...], preferred_element_type=F32, precision=HIGHEST)


def _suffix(lf_rows, u2):
    n = lf_rows.shape[0]
    blk = math.gcd(n, SUFFIX_ROWS)
    return pl.pallas_call(
        _suffix_kernel, grid=(n // blk,),
        in_specs=[pl.BlockSpec((blk, PAGE), lambda i: (i, 0)), _full((PAGE, 2 * PAGE))],
        out_specs=pl.BlockSpec((blk, 2 * PAGE), lambda i: (i, 0)),
        out_shape=jax.ShapeDtypeStruct((n, 2 * PAGE), F32),
        compiler_params=_cparams(("arbitrary",)), name="logf_suffix",
    )(lf_rows, u2)


def _decode_kernel(n_steps, pt_ref, qbd_ref, kn_ref, vn_ref, lfn_ref, uincl_ref, *rest):
    del pt_ref
    g_pages = PAGES_PER_STEP
    kc_refs = rest[0:g_pages]
    vc_refs = rest[g_pages:2 * g_pages]
    ds_refs = rest[2 * g_pages:3 * g_pages]
    o_ref, m_ref, l_ref, acc_ref, carry_ref, cn_ref = rest[3 * g_pages:]
    p = pl.program_id(1)
    rows = qbd_ref.shape[1]
    nh = ds_refs[0].shape[1]
    s_new = rows // nh

    def expand(t):
        return jnp.concatenate(
            [jnp.broadcast_to(t[hh:hh + 1, :], (s_new, t.shape[1])) for hh in range(nh)], axis=0)

    @pl.when(p == 0)
    def _():
        m_ref[...] = jnp.full_like(m_ref, NEG_BIG)
        l_ref[...] = jnp.zeros_like(l_ref)
        acc_ref[...] = jnp.zeros_like(acc_ref)
        carry_ref[...] = jnp.zeros_like(carry_ref)
        cn_ref[...] = jnp.dot(lfn_ref[0], uincl_ref[...], preferred_element_type=F32,
                              precision=HIGHEST)

    q = qbd_ref[0]
    cnx = expand(cn_ref[...])
    lane = lax.broadcasted_iota(jnp.int32, (rows, PAGE), 1)
    tq = lax.broadcasted_iota(jnp.int32, (rows, PAGE), 0) % s_new
    cn_col = jnp.sum(jnp.where(lane == tq, cnx, 0.0), axis=1, keepdims=True)

    def attend(kts, vts, biases, keep):
        ss = [jnp.dot(q, kt, preferred_element_type=F32) + b for kt, b in zip(kts, biases)]
        if keep is not None:
            ss = [jnp.where(keep, s, -jnp.inf) for s in ss]
        smax = functools.reduce(jnp.maximum, ss)
        m_prev = m_ref[...]
        m_new = jnp.maximum(m_prev, jnp.max(smax, axis=1, keepdims=True))
        alpha = jnp.exp(m_prev - m_new)
        prs = [jnp.exp(s - m_new) for s in ss]
        psum = functools.reduce(jnp.add, prs)
        l_ref[...] = alpha * l_ref[...] + jnp.sum(psum, axis=1, keepdims=True)
        pvs = [lax.dot_general(pr.astype(BF16), vt, (((1,), (1,)), ((), ())),
                               preferred_element_type=F32) for pr, vt in zip(prs, vts)]
        acc_ref[...] = alpha * acc_ref[...] + functools.reduce(jnp.add, pvs)
        m_ref[...] = m_new

    carry = carry_ref[...]
    biases = []
    for g in range(g_pages):
        ds = ds_refs[g][0]
        biases.append(expand(ds[:, :PAGE] + carry) + cn_col)
        carry = carry + ds[:, PAGE:PAGE + 1]
    carry_ref[...] = carry
    attend([r[...].astype(BF16) for r in kc_refs], [r[...].astype(BF16) for r in vc_refs],
           biases, None)

    @pl.when(p == n_steps - 1)
    def _():
        attend([kn_ref[0]], [vn_ref[0]], [cn_col - cnx], lane <= tq)
        out = acc_ref[...] / l_ref[...]
        width = out.shape[1]
        r = lax.broadcasted_iota(jnp.int32, (rows, width), 0) // s_new
        cidx = lax.broadcasted_iota(jnp.int32, (rows, width), 1) // HEAD_DIM
        out = jnp.where(r == cidx, out, 0.0)
        o_ref[0] = jnp.sum(out.reshape(nh, s_new, width), axis=0).astype(o_ref.dtype)


def _decode(pt_flat, qbd, kn_pad, vn_pad, lfn_t, kc, vc, dsuf, uincl, n_pages):
    bd_, rows, db = qbd.shape
    nh = dsuf.shape[1]
    s_new = rows // nh
    g_pages = PAGES_PER_STEP
    n_steps = n_pages // g_pages
    seqb = lambda b, p, pt: (b, 0, 0)

    def page(g):
        return lambda b, p, pt: (pt[b * n_pages + (n_pages - 1 - (p * g_pages + g))], 0, 0)

    in_specs = [
        pl.BlockSpec((1, rows, db), seqb),
        pl.BlockSpec((1, db, PAGE), seqb),
        pl.BlockSpec((1, db, PAGE), seqb),
        pl.BlockSpec((1, nh, PAGE), seqb),
        pl.BlockSpec((PAGE, PAGE), lambda b, p, pt: (0, 0)),
    ]
    in_specs += [pl.BlockSpec((None, db, PAGE), page(g)) for g in range(g_pages)]
    in_specs += [pl.BlockSpec((None, db, PAGE), page(g)) for g in range(g_pages)]
    in_specs += [pl.BlockSpec((1, nh, 2 * PAGE), page(g)) for g in range(g_pages)]
    grid_spec = pltpu.PrefetchScalarGridSpec(
        num_scalar_prefetch=1, grid=(bd_, n_steps), in_specs=in_specs,
        out_specs=pl.BlockSpec((1, s_new, db), seqb),
        scratch_shapes=[pltpu.VMEM((rows, 1), F32), pltpu.VMEM((rows, 1), F32),
                        pltpu.VMEM((rows, db), F32), pltpu.VMEM((nh, 1), F32),
                        pltpu.VMEM((nh, PAGE), F32)],
    )
    return pl.pallas_call(
        functools.partial(_decode_kernel, n_steps), grid_spec=grid_spec,
        out_shape=jax.ShapeDtypeStruct((bd_, s_new, db), BF16),
        compiler_params=_cparams(("arbitrary", "arbitrary")), name="fox_decode",
    )(pt_flat, qbd, kn_pad, vn_pad, lfn_t, uincl, *([kc] * g_pages), *([vc] * g_pages),
      *([dsuf] * g_pages))


def _outproj_kernel(n_prompt_tiles, x_ref, oa_ref, obp_ref, obs_ref, woa_ref, wob_ref, g2_ref,
                    wr_ref, br_ref, x1_ref, h2_ref, ti_ref, tg_ref):
    i = pl.program_id(0)
    ob = jnp.where(i >= n_prompt_tiles, obs_ref[...], obp_ref[...])
    mix = (jnp.dot(oa_ref[...], woa_ref[...], preferred_element_type=F32)
           + jnp.dot(ob, wob_ref[...], preferred_element_type=F32))
    x1 = x_ref[...] + mix
    x1_ref[...] = x1
    ms = jnp.mean(x1 * x1, axis=-1, keepdims=True)
    h2 = x1 * lax.rsqrt(ms + EPS) * g2_ref[...]
    h2_ref[...] = h2
    logits = jnp.dot(h2, wr_ref[...], preferred_element_type=F32, precision=HIGHEST) + br_ref[...]
    lane = lax.broadcasted_iota(jnp.int32, logits.shape, 1)
    cur = jnp.where(lane < N_EXPERTS, logits, -jnp.inf)
    idx_out = jnp.zeros(logits.shape, jnp.int32)
    val_out = jnp.zeros(logits.shape, F32)
    v0 = None
    denom = None
    for k in range(TOP_K):
        mval = jnp.max(cur, axis=1, keepdims=True)
        midx = jnp.min(jnp.where(cur == mval, lane, LANES), axis=1, keepdims=True)
        if k == 0:
            v0 = mval
            ek = jnp.ones_like(mval)
            denom = ek
        else:
            ek = jnp.exp(mval - v0)
            denom = denom + ek
        idx_out = jnp.where(lane == k, midx, idx_out)
        val_out = jnp.where(lane == k, ek, val_out)
        cur = jnp.where(lane == midx, -jnp.inf, cur)
    ti_ref[...] = idx_out
    tg_ref[...] = val_out / denom


def _outproj(x, oa, obp, obs, woa, wob, g2, wr, br, n_prompt_tiles):
    nt, d = x.shape
    half = oa.shape[1]
    row = lambda w: pl.BlockSpec((TM, w), lambda i: (i, 0))
    in_specs = [
        row(d), row(half),
        pl.BlockSpec((TM, half), lambda i: (jnp.minimum(i, n_prompt_tiles - 1), 0)),
        _full((TM, half)),
        _full((half, d)), _full((half, d)), _full((1, d)), _full((d, LANES)), _full((1, LANES)),
    ]
    out_shape = [jax.ShapeDtypeStruct((nt, d), F32), jax.ShapeDtypeStruct((nt, d), F32),
                 jax.ShapeDtypeStruct((nt, LANES), jnp.int32), jax.ShapeDtypeStruct((nt, LANES), F32)]
    return pl.pallas_call(
        functools.partial(_outproj_kernel, n_prompt_tiles),
        grid=(nt // TM,), in_specs=in_specs,
        out_specs=[row(d), row(d), row(LANES), row(LANES)], out_shape=out_shape,
        compiler_params=_cparams(("arbitrary",)), name="outproj_router",
    )(x, oa, obp, obs, woa, wob, g2, wr, br)


def _start_row_gather(idx_ref, base, n_rows, src_hbm, dst_ref, sem):
    def body(r, carry):
        tok = idx_ref[base + r]
        pltpu.make_async_copy(src_hbm.at[pl.ds(tok, 1)], dst_ref.at[pl.ds(r, 1)], sem).start()
        return carry
    lax.fori_loop(0, n_rows, body, 0, unroll=8)


def _wait_row_gather(n_rows, src_hbm, dst_ref, sem):
    pltpu.make_async_copy(src_hbm.at[pl.ds(0, n_rows)], dst_ref, sem).wait()


def _expert_kernel(be_ref, bv_ref, st_ref, h2_hbm, wgu_ref, bgu_ref, wdn_ref, bdn_ref,
                   ys_ref, xs_buf, wgu_bf, wdn_bf, sems):
    i = pl.program_id(0)
    n_blocks = pl.num_programs(0)
    rows = xs_buf.shape[1]
    d_ff = wdn_ref.shape[1]
    slot = i % 2

    @pl.when(bv_ref[i] == 0)
    def _():
        ys_ref[...] = jnp.zeros_like(ys_ref)

    @pl.when((i == 0) & (bv_ref[0] == 1))
    def _():
        _start_row_gather(st_ref, 0, rows, h2_hbm, xs_buf.at[0], sems.at[0])

    @pl.when(bv_ref[i] == 1)
    def _():
        nxt = jnp.minimum(i + 1, n_blocks - 1)

        @pl.when((i + 1 < n_blocks) & (bv_ref[nxt] == 1))
        def _():
            _start_row_gather(st_ref, nxt * rows, rows, h2_hbm, xs_buf.at[1 - slot],
                              sems.at[1 - slot])

        prev = be_ref[jnp.maximum(i - 1, 0)]

        @pl.when((i == 0) | (prev != be_ref[i]))
        def _():
            wgu_bf[...] = wgu_ref[0].astype(BF16)
            wdn_bf[...] = wdn_ref[0].astype(BF16)

        _wait_row_gather(rows, h2_hbm, xs_buf.at[slot], sems.at[slot])
        xs = xs_buf[slot].astype(BF16)
        gu = jnp.dot(xs, wgu_bf[...], preferred_element_type=F32) + bgu_ref[0]
        glu = jnp.minimum(gu[:, :d_ff], SWIGLU_LIMIT)
        lin = jnp.clip(gu[:, d_ff:], -SWIGLU_LIMIT, SWIGLU_LIMIT)
        act = glu * jax.nn.sigmoid(SWIGLU_ALPHA * glu) * (lin + 1.0)
        ys_ref[...] = jnp.dot(act.astype(BF16), wdn_bf[...], preferred_element_type=F32) + bdn_ref[0]


def _experts(block_e, block_valid, slot_tok, h2, wgu, bgu, wdn, bdn):
    n_slots = slot_tok.shape[0]
    n_blocks = n_slots // MOE_BLOCK
    d = h2.shape[1]
    _, _, two_ff = wgu.shape
    d_ff = two_ff // 2
    eidx = lambda i, be, bv, st: (be[i], 0, 0)
    grid_spec = pltpu.PrefetchScalarGridSpec(
        num_scalar_prefetch=3, grid=(n_blocks,),
        in_specs=[
            pl.BlockSpec(memory_space=pl.ANY),
            pl.BlockSpec((1, d, two_ff), eidx),
            pl.BlockSpec((1, 1, two_ff), eidx),
            pl.BlockSpec((1, d_ff, d), eidx),
            pl.BlockSpec((1, 1, d), eidx),
        ],
        out_specs=pl.BlockSpec((MOE_BLOCK, d), lambda i, be, bv, st: (i, 0)),
        scratch_shapes=[pltpu.VMEM((2, MOE_BLOCK, d), F32), pltpu.VMEM((d, two_ff), BF16),
                        pltpu.VMEM((d_ff, d), BF16), pltpu.SemaphoreType.DMA((2,))],
    )
    return pl.pallas_call(
        _expert_kernel, grid_spec=grid_spec,
        out_shape=jax.ShapeDtypeStruct((n_slots, d), F32),
        compiler_params=_cparams(("arbitrary",)), name="experts",
    )(block_e, block_valid, slot_tok, h2, wgu, bgu, wdn, bdn)


def _combine_kernel(dest_ref, x1_ref, tg_ref, ys_hbm, o_ref, buf, sems):
    i = pl.program_id(0)
    n_tiles = pl.num_programs(0)
    rows = x1_ref.shape[0]
    n = TOP_K * rows
    slot = i % 2

    @pl.when(i == 0)
    def _():
        _start_row_gather(dest_ref, 0, n, ys_hbm, buf.at[0], sems.at[0])

    @pl.when(i + 1 < n_tiles)
    def _():
        _start_row_gather(dest_ref, (i + 1) * n, n, ys_hbm, buf.at[1 - slot], sems.at[1 - slot])

    _wait_row_gather(n, ys_hbm, buf.at[slot], sems.at[slot])
    acc = x1_ref[...]
    gates = tg_ref[...]
    for k in range(TOP_K):
        acc = acc + buf[slot, k * rows:(k + 1) * rows, :] * gates[:, k:k + 1]
    o_ref[...] = acc


def _combine(dest_flat, x1, tg, ys):
    nt, d = x1.shape
    grid_spec = pltpu.PrefetchScalarGridSpec(
        num_scalar_prefetch=1, grid=(nt // TM,),
        in_specs=[pl.BlockSpec((TM, d), lambda i, dr: (i, 0)),
                  pl.BlockSpec((TM, LANES), lambda i, dr: (i, 0)),
                  pl.BlockSpec(memory_space=pl.ANY)],
        out_specs=pl.BlockSpec((TM, d), lambda i, dr: (i, 0)),
        scratch_shapes=[pltpu.VMEM((2, TOP_K * TM, d), F32), pltpu.SemaphoreType.DMA((2,))],
    )
    return pl.pallas_call(
        _combine_kernel, grid_spec=grid_spec,
        out_shape=jax.ShapeDtypeStruct((nt, d), F32),
        compiler_params=_cparams(("arbitrary",)), name="combine",
    )(dest_flat, x1, tg, ys)


def _route(top_idx):
    nt = top_idx.shape[0]
    m = nt * TOP_K
    e_flat = top_idx.reshape(m)
    onehot = (e_flat[:, None] == jnp.arange(N_EXPERTS, dtype=jnp.int32)[None, :]).astype(jnp.int32)
    csum = jnp.cumsum(onehot, axis=0)
    sizes = csum[-1]
    pos = jnp.sum(jnp.where(onehot == 1, csum - 1, 0), axis=1)
    psizes = (sizes + MOE_BLOCK - 1) // MOE_BLOCK * MOE_BLOCK
    pend = jnp.cumsum(psizes)
    pstart = pend - psizes
    dest = (jnp.sum(onehot * pstart[None, :], axis=1) + pos).astype(jnp.int32)
    n_blocks = -(-(m + N_EXPERTS * (MOE_BLOCK - 1)) // MOE_BLOCK)
    n_slots = n_blocks * MOE_BLOCK
    tok = jnp.arange(m, dtype=jnp.int32) // TOP_K
    slot_tok = jnp.zeros((n_slots,), jnp.int32).at[dest].set(tok)
    bstart = jnp.arange(n_blocks, dtype=jnp.int32) * MOE_BLOCK
    block_e = jnp.minimum(jnp.sum((pend[None, :] <= bstart[:, None]).astype(jnp.int32), axis=1),
                          N_EXPERTS - 1).astype(jnp.int32)
    block_valid = (bstart < pend[-1]).astype(jnp.int32)
    return dest, slot_tok, block_e, block_valid


def _bias_selector(nh, db):
    rows = jnp.arange(LANES)[:, None]
    cols = jnp.arange(2 * db)[None, :]
    head = cols // LANES
    off = cols % LANES - BIAS_LANE
    part = rows // nh
    is_split = (part < N_SPLIT) & (rows % nh == head)
    is_one = rows == N_SPLIT * nh
    q_sel = ((is_split & (off == part)) | (is_one & (off >= N_SPLIT) & (off < 2 * N_SPLIT)))
    k_sel = (is_one & (off >= 0) & (off < N_SPLIT)).astype(F32) \
        - (is_split & (off == part + N_SPLIT)).astype(F32)
    return jnp.concatenate([q_sel.astype(F32), k_sel], axis=1).astype(BF16)


def kernel(x_prompt, x_sample, cache_k, cache_v, cache_logf, page_table, norm1_g, w_in, b_f, gv_g,
           w_s, b_s, q_g, k_g, w_out, norm2_g, w_router, b_router, w_gu, b_gu, w_dn, b_dn):
    batch, seq, d = x_prompt.shape
    dec_b, dec_s, _ = x_sample.shape
    depth = w_in.shape[0]
    ha = gv_g.shape[1]
    hb = b_f.shape[1]
    da, db = ha * HEAD_DIM, hb * HEAD_DIM
    n_main = 2 * da + 3 * db
    n_phys = cache_k.shape[1]
    n_pages = page_table.shape[1]
    t_prompt = batch * seq
    t_dec = dec_b * dec_s
    assert t_dec == TM and seq % TM == 0 and TM % CHUNK == 0 and seq % BQ == 0 and BQ == BK
    assert da % LANES == 0 and db % LANES == 0 and dec_s <= CHUNK and TM % dec_s == 0
    assert cache_k.shape[2] == PAGE and n_pages % PAGES_PER_STEP == 0
    assert (N_SPLIT + 1) * hb <= LANES
    n_prompt_tiles = t_prompt // TM
    tiles_per_seq = seq // TM

    x = jnp.concatenate([x_prompt.reshape(t_prompt, d), x_sample.reshape(t_dec, d)], axis=0)

    wm_all = w_in[:, :, :n_main].astype(BF16)
    wf_all = jnp.pad(w_in[:, :, n_main:], ((0, 0), (0, 0), (0, LANES - hb))).astype(BF16)
    bf_all = jnp.pad(b_f, ((0, 0), (0, LANES - hb))).reshape(depth, 1, LANES)
    gv_all = gv_g.reshape(depth, 1, da)
    qg_all = jnp.tile(q_g, (1, hb)).reshape(depth, 1, db)
    kg_all = jnp.tile(k_g, (1, hb)).reshape(depth, 1, db)
    head_of = jnp.arange(db) // HEAD_DIM
    bd = (head_of[:, None] == head_of[None, :]).astype(BF16)
    tril = jnp.tril(jnp.ones((CHUNK, CHUNK), F32))
    wt = w_s * tril
    eye_p = jnp.eye(TM // CHUNK, dtype=F32)
    eye_s = jnp.eye(TM // dec_s, dtype=F32)
    wblk_p = jnp.einsum('ab,lhts->lhatbs', eye_p, wt).reshape(depth, ha, TM, TM)
    wblk_s = jnp.einsum('ab,lhts->lhatbs', eye_s, wt[:, :, :dec_s, :dec_s]).reshape(depth, ha, TM, TM)
    wblk_all = jnp.stack([wblk_p, wblk_s], axis=1).astype(BF16)
    bs_t = jnp.repeat(jnp.swapaxes(b_s, 1, 2), HEAD_DIM, axis=2)
    bias_p = jnp.tile(bs_t, (1, TM // CHUNK, 1))
    bias_s = jnp.tile(bs_t[:, :dec_s], (1, TM // dec_s, 1))
    bias_all = jnp.stack([bias_p, bias_s], axis=1)
    ltri = jnp.tril(jnp.ones((TM, TM), F32))
    sel = _bias_selector(hb, db)
    woa_all = w_out[:, :da].astype(BF16)
    wob_all = w_out[:, da:].astype(BF16)
    wr_all = jnp.pad(w_router, ((0, 0), (0, 0), (0, LANES - N_EXPERTS)))
    br_all = jnp.pad(b_router, ((0, 0), (0, LANES - N_EXPERTS))).reshape(depth, 1, LANES)
    two_ff = w_gu.shape[3]
    wgu_all = w_gu.reshape(depth * N_EXPERTS, d, two_ff)
    bgu_all = b_gu.reshape(depth * N_EXPERTS, 1, two_ff)
    wdn_all = w_dn.reshape(depth * N_EXPERTS, two_ff // 2, d)
    bdn_all = b_dn.reshape(depth * N_EXPERTS, 1, d)

    kc = jnp.transpose(cache_k, (0, 1, 3, 4, 2)).reshape(depth * n_phys, db, PAGE)
    vc = jnp.transpose(cache_v, (0, 1, 3, 4, 2)).reshape(depth * n_phys, db, PAGE)
    lfc_rows = jnp.swapaxes(cache_logf, 2, 3).reshape(depth * n_phys * hb, PAGE)
    pt = page_table.reshape(dec_b * n_pages).astype(jnp.int32)
    ar = jnp.arange(PAGE)
    ustrict = (ar[:, None] > ar[None, :]).astype(F32)
    uincl = (ar[:, None] <= ar[None, :]).astype(F32)
    u2 = jnp.concatenate([ustrict, jnp.ones((PAGE, PAGE), F32)], axis=1)
    dsuf = _suffix(lfc_rows, u2).reshape(depth * n_phys, hb, 2 * PAGE)
    eye_h = jnp.eye(hb, dtype=F32)

    kp_l, vp_l, fp_l, ks_l, vs_l, fs_l, cv_l = [], [], [], [], [], [], []
    for l in range(depth):
        oa, qa, kt, va, k32, v32, logf, va_s, q_s = _project(
            x, norm1_g[l].reshape(1, d), wm_all[l], wf_all[l], bf_all[l], gv_all[l], qg_all[l],
            kg_all[l], bd, wblk_all[l], bias_all[l], ltri, sel, n_prompt_tiles, tiles_per_seq)

        ob_p = _flash(qa, kt, va, batch, seq)

        q4 = q_s.reshape(dec_b, dec_s, hb, HEAD_DIM) * (HEAD_DIM ** -0.5)
        qbd = jnp.einsum('bthd,hg->bhtgd', q4, eye_h).reshape(dec_b, hb * dec_s, db).astype(BF16)
        k_new = k32[t_prompt:].reshape(dec_b, dec_s, db)
        v_new = v32[t_prompt:].reshape(dec_b, dec_s, db)
        lf_new = logf[t_prompt:].reshape(dec_b, dec_s, hb)
        padr = ((0, 0), (0, 0), (0, PAGE - dec_s))
        kn_pad = jnp.pad(jnp.swapaxes(k_new, 1, 2), padr).astype(BF16)
        vn_pad = jnp.pad(jnp.swapaxes(v_new, 1, 2), padr).astype(BF16)
        lfn_t = jnp.pad(jnp.swapaxes(lf_new, 1, 2), ((0, 0), (0, 0), (0, PAGE - dec_s)))
        ob_s = _decode(pt + l * n_phys, qbd, kn_pad, vn_pad, lfn_t, kc, vc, dsuf, uincl,
                       n_pages).reshape(t_dec, db)

        x1, h2, ti, tg = _outproj(x, oa, ob_p, ob_s, woa_all[l], wob_all[l],
                                  norm2_g[l].reshape(1, d), wr_all[l], br_all[l], n_prompt_tiles)
        dest, slot_tok, block_e, block_valid = _route(ti[:, :TOP_K])
        ys = _experts(block_e + l * N_EXPERTS, block_valid, slot_tok, h2, wgu_all, bgu_all,
                      wdn_all, bdn_all)
        nt = x1.shape[0]
        dest_t = dest.reshape(nt // TM, TM, TOP_K).transpose(0, 2, 1).reshape(nt * TOP_K)
        x = _combine(dest_t, x1, tg, ys)

        kp_l.append(k32[:t_prompt].reshape(batch, seq, hb, HEAD_DIM))
        vp_l.append(v32[:t_prompt].reshape(batch, seq, hb, HEAD_DIM))
        fp_l.append(logf[:t_prompt].reshape(batch, seq, hb))
        ks_l.append(k_new.reshape(dec_b, dec_s, hb, HEAD_DIM))
        vs_l.append(v_new.reshape(dec_b, dec_s, hb, HEAD_DIM))
        fs_l.append(lf_new)
        cv_l.append(va_s.reshape(dec_b, dec_s, ha, HEAD_DIM))

    y_prompt = x[:t_prompt].reshape(batch, seq, d)
    y_sample = x[t_prompt:].reshape(dec_b, dec_s, d)
    return (y_prompt, y_sample, jnp.stack(kp_l), jnp.stack(vp_l), jnp.stack(fp_l),
            jnp.stack(ks_l), jnp.stack(vs_l), jnp.stack(fs_l), jnp.stack(cv_l))
```

```python
import functools
import math

import jax
import jax.numpy as jnp
from jax import lax
from jax.experimental import pallas as pl
from jax.experimental.pallas import tpu as pltpu

F32 = jnp.float32
BF16 = jnp.bfloat16
HIGHEST = lax.Precision.HIGHEST

HEAD_DIM = 64
CHUNK = 128
PAGE = 128
N_EXPERTS = 32
TOP_K = 4
EPS = 1e-6
SWIGLU_ALPHA = 1.702
SWIGLU_LIMIT = 7.0
LOG2E = math.log2(math.e)
NEG_BIG = -1e30

LANES = 128
TM = 256
BQ = 512
BK = 512
QC = 128
KC = 256
PAGES_PER_STEP = 8
SUFFIX_ROWS = 4096
MOE_BLOCK = 256
GRANULE = 8
VMEM_LIMIT = 56 * 1024 * 1024

BIAS_LANE = HEAD_DIM
N_SPLIT = 3


def _cparams(sem):
    return pltpu.CompilerParams(dimension_semantics=sem, vmem_limit_bytes=VMEM_LIMIT)


def _full(shape):
    nd = len(shape)
    return pl.BlockSpec(shape, lambda *_: (0,) * nd)


def _proj_kernel(tiles_per_seq,
                 x_ref, g1_ref, wm_ref, wf_ref, bf_ref, gv_ref, qg_ref, kg_ref, bd_ref,
                 wblk_ref, bias_ref, ltri_ref, sel_ref,
                 oa_ref, qa_ref, kt_ref, va_ref, k32_ref, v32_ref, logf_ref,
                 vn_ref, qs_ref, carry_ref):
    i = pl.program_id(0)
    da = gv_ref.shape[1]
    db = qg_ref.shape[1]
    x = x_ref[...]
    ms = jnp.mean(x * x, axis=-1, keepdims=True)
    h = (x * lax.rsqrt(ms + EPS) * g1_ref[...]).astype(BF16)
    p = jnp.dot(h, wm_ref[...], preferred_element_type=F32)
    fl = jnp.dot(h, wf_ref[...], preferred_element_type=F32)

    def group_norm(t, gamma):
        sq = (t * t).astype(BF16)
        gms = jnp.dot(sq, bd_ref[...], preferred_element_type=F32) * (1.0 / HEAD_DIM)
        return t * lax.rsqrt(gms + EPS) * gamma

    u = jax.nn.gelu(p[:, 0:da])
    van = group_norm(jax.nn.gelu(p[:, da:2 * da]), gv_ref[...])
    qn = group_norm(p[:, 2 * da:2 * da + db], qg_ref[...])
    kn = group_norm(p[:, 2 * da + db:2 * da + 2 * db], kg_ref[...])
    vv = p[:, 2 * da + 2 * db:2 * da + 3 * db]

    vn_ref[...] = van
    qs_ref[...] = qn
    k32_ref[...] = kn
    v32_ref[...] = vv

    lane = lax.broadcasted_iota(jnp.int32, (x.shape[0], LANES), 1)
    low = lane < HEAD_DIM

    van_bf = van.astype(BF16)
    for j in range(da // LANES):
        rhs = van_bf[:, j * LANES:(j + 1) * LANES]
        z0 = jnp.dot(wblk_ref[0, 2 * j], rhs, preferred_element_type=F32)
        z1 = jnp.dot(wblk_ref[0, 2 * j + 1], rhs, preferred_element_type=F32)
        z = jnp.where(low, z0, z1) + bias_ref[0, :, j * LANES:(j + 1) * LANES]
        oa_ref[:, j * LANES:(j + 1) * LANES] = (u[:, j * LANES:(j + 1) * LANES] * z).astype(BF16)

    nh = logf_ref.shape[1]
    z = fl + bf_ref[...]
    lf = jnp.minimum(z, 0.0) - jnp.log1p(jnp.exp(-jnp.abs(z)))
    lf = jnp.where(lane < nh, lf, 0.0)
    logf_ref[...] = lf[:, :nh]

    @pl.when(i % tiles_per_seq == 0)
    def _():
        carry_ref[...] = jnp.zeros_like(carry_ref)

    c = jnp.dot(ltri_ref[...], lf, preferred_element_type=F32, precision=HIGHEST) + carry_ref[...]
    carry_ref[...] = c[c.shape[0] - 1:, :]

    c2 = c * LOG2E
    hi = c2.astype(BF16).astype(F32)
    r1 = c2 - hi
    mid = r1.astype(BF16).astype(F32)
    lo = (r1 - mid).astype(BF16).astype(F32)
    c3 = (hi + pltpu.roll(mid, nh, axis=1) + pltpu.roll(lo, 2 * nh, axis=1)
          + jnp.where(lane == N_SPLIT * nh, 1.0, 0.0)).astype(BF16)
    extra = jnp.dot(c3, sel_ref[...], preferred_element_type=F32)

    qscale = (HEAD_DIM ** -0.5) * LOG2E
    kparts = []
    for j in range(db // LANES):
        qp = qn[:, j * LANES:(j + 1) * LANES] * qscale
        kp = kn[:, j * LANES:(j + 1) * LANES]
        vp = vv[:, j * LANES:(j + 1) * LANES]
        for half, (qh, kh, vh) in enumerate(((qp, kp, vp),
                                             (pltpu.roll(qp, HEAD_DIM, axis=1),
                                              pltpu.roll(kp, HEAD_DIM, axis=1),
                                              pltpu.roll(vp, HEAD_DIM, axis=1)))):
            hh = 2 * j + half
            cols = slice(hh * LANES, (hh + 1) * LANES)
            kcols = slice(2 * db + hh * LANES, 2 * db + (hh + 1) * LANES)
            qa_ref[:, cols] = jnp.where(low, qh, extra[:, cols]).astype(BF16)
            va_ref[:, cols] = jnp.where(low, vh, 1.0).astype(BF16)
            kparts.append(jnp.where(low, kh, extra[:, kcols]))
    kt_ref[...] = jnp.concatenate(kparts, axis=1).T.astype(BF16)


def _project(x, g1, wm, wf, bfp, gv, qg, kg, bd, wblk, bias, ltri, sel, n_prompt_tiles, tiles_per_seq):
    nt, d = x.shape
    n_main = wm.shape[1]
    da, db = gv.shape[1], qg.shape[1]
    nh = db // HEAD_DIM
    grid = (nt // TM,)
    row = lambda w: pl.BlockSpec((TM, w), lambda i: (i, 0))
    variant = lambda i: i // n_prompt_tiles
    in_specs = [
        row(d), _full((1, d)), _full((d, n_main)), _full((d, LANES)), _full((1, LANES)),
        _full((1, da)), _full((1, db)), _full((1, db)), _full((db, db)),
        pl.BlockSpec((1, da // HEAD_DIM, TM, TM), lambda i: (variant(i), 0, 0, 0)),
        pl.BlockSpec((1, TM, da), lambda i: (variant(i), 0, 0)),
        _full((TM, TM)), _full((LANES, 4 * db)),
    ]
    out_shape = [
        jax.ShapeDtypeStruct((nt, da), BF16),
        jax.ShapeDtypeStruct((nt, 2 * db), BF16),
        jax.ShapeDtypeStruct((2 * db, nt), BF16),
        jax.ShapeDtypeStruct((nt, 2 * db), BF16),
        jax.ShapeDtypeStruct((nt, db), F32),
        jax.ShapeDtypeStruct((nt, db), F32),
        jax.ShapeDtypeStruct((nt, nh), F32),
        jax.ShapeDtypeStruct((TM, da), F32),
        jax.ShapeDtypeStruct((TM, db), F32),
    ]
    out_specs = [row(da), row(2 * db), pl.BlockSpec((2 * db, TM), lambda i: (0, i)), row(2 * db),
                 row(db), row(db), row(nh), _full((TM, da)), _full((TM, db))]
    return pl.pallas_call(
        functools.partial(_proj_kernel, tiles_per_seq),
        grid=grid, in_specs=in_specs, out_specs=out_specs, out_shape=out_shape,
        scratch_shapes=[pltpu.VMEM((1, LANES), F32)],
        compiler_params=_cparams(("arbitrary",)), name="proj",
    )(x, g1, wm, wf, bfp, gv, qg, kg, bd, wblk, bias, ltri, sel)


def _flash_kernel(qa_ref, kt_ref, va_ref, o_ref, m_ref, acc_ref):
    qi = pl.program_id(1)
    ki = pl.program_id(2)
    nh = va_ref.shape[1] // LANES

    @pl.when(ki == 0)
    def _():
        m_ref[...] = jnp.full_like(m_ref, NEG_BIG)
        acc_ref[...] = jnp.zeros_like(acc_ref)

    def step(diag):
        for hh in range(nh):
            for qc in range(BQ // QC):
                rows = slice(qc * QC, (qc + 1) * QC)
                q = qa_ref[rows, hh * LANES:(hh + 1) * LANES]
                m = m_ref[hh, rows, :]
                acc = acc_ref[hh, rows, :]
                for kc in range(BK // KC):
                    if diag and kc * KC > qc * QC + QC - 1:
                        continue
                    cols = slice(kc * KC, (kc + 1) * KC)
                    s = jnp.dot(q, kt_ref[hh * LANES:(hh + 1) * LANES, cols],
                                preferred_element_type=F32)
                    if diag and kc * KC + KC - 1 > qc * QC:
                        r = lax.broadcasted_iota(jnp.int32, (QC, KC), 0) + qc * QC
                        c = lax.broadcasted_iota(jnp.int32, (QC, KC), 1) + kc * KC
                        s = jnp.where(c <= r, s, -jnp.inf)
                    smax = jnp.maximum(s[:, :LANES], s[:, LANES:])
                    m_new = jnp.maximum(m, jnp.max(smax, axis=1, keepdims=True))
                    alpha = jnp.exp2(m - m_new)
                    pr = jnp.concatenate([jnp.exp2(s[:, :LANES] - m_new),
                                          jnp.exp2(s[:, LANES:] - m_new)], axis=1)
                    acc = alpha * acc + jnp.dot(pr.astype(BF16),
                                                va_ref[cols, hh * LANES:(hh + 1) * LANES],
                                                preferred_element_type=F32)
                    m = m_new
                m_ref[hh, rows, :] = m
                acc_ref[hh, rows, :] = acc

    @pl.when(ki < qi)
    def _():
        step(False)

    @pl.when(ki == qi)
    def _():
        step(True)
        lane = lax.broadcasted_iota(jnp.int32, (BQ, LANES), 1)
        for j in range(nh // 2):
            a0 = acc_ref[2 * j]
            a1 = acc_ref[2 * j + 1]
            num = jnp.where(lane < HEAD_DIM, a0, pltpu.roll(a1, HEAD_DIM, axis=1))
            den = jnp.where(lane < HEAD_DIM, pltpu.roll(a0, HEAD_DIM, axis=1), a1)
            o_ref[:, j * LANES:(j + 1) * LANES] = (num / den).astype(o_ref.dtype)


def _flash(qa, kt, va, batch, seq):
    wide = qa.shape[1]
    db = wide // 2
    nh = db // HEAD_DIM
    nq = seq // BQ
    nk = seq // BK
    qrow = lambda b, qi, ki: (b * nq + qi, 0)
    in_specs = [
        pl.BlockSpec((BQ, wide), qrow),
        pl.BlockSpec((wide, BK), lambda b, qi, ki: (0, b * nk + jnp.minimum(ki, qi))),
        pl.BlockSpec((BK, wide), lambda b, qi, ki: (b * nk + jnp.minimum(ki, qi), 0)),
    ]
    return pl.pallas_call(
        _flash_kernel,
        grid=(batch, nq, nk), in_specs=in_specs,
        out_specs=pl.BlockSpec((BQ, db), qrow),
        out_shape=jax.ShapeDtypeStruct((batch * seq, db), BF16),
        scratch_shapes=[pltpu.VMEM((nh, BQ, LANES), F32), pltpu.VMEM((nh, BQ, LANES), F32)],
        compiler_params=_cparams(("arbitrary", "arbitrary", "arbitrary")), name="fox_prompt",
    )(qa, kt, va)


def _suffix_kernel(lf_ref, u2_ref, o_ref):
    o_ref[...] = jnp.dot(lf_ref[...], u2_ref[...], preferred_element_type=F32, precision=HIGHEST)


def _suffix(lf_rows, u2):
    n = lf_rows.shape[0]
    blk = math.gcd(n, SUFFIX_ROWS)
    return pl.pallas_call(
        _suffix_kernel, grid=(n // blk,),
        in_specs=[pl.BlockSpec((blk, PAGE), lambda i: (i, 0)), _full((PAGE, 2 * PAGE))],
        out_specs=pl.BlockSpec((blk, 2 * PAGE), lambda i: (i, 0)),
        out_shape=jax.ShapeDtypeStruct((n, 2 * PAGE), F32),
        compiler_params=_cparams(("arbitrary",)), name="logf_suffix",
    )(lf_rows, u2)


def _decode_kernel(n_steps, pt_ref, qbd_ref, kn_ref, vn_ref, lfn_ref, uincl_ref, *rest):
    del pt_ref
    g_pages = PAGES_PER_STEP
    kc_refs = rest[0:g_pages]
    vc_refs = rest[g_pages:2 * g_pages]
    ds_refs = rest[2 * g_pages:3 * g_pages]
    o_ref, m_ref, l_ref, acc_ref, carry_ref, cn_ref = rest[3 * g_pages:]
    p = pl.program_id(1)
    rows = qbd_ref.shape[1]
    nh = ds_refs[0].shape[1]
    s_new = rows // nh

    def expand(t):
        return jnp.concatenate(
            [jnp.broadcast_to(t[hh:hh + 1, :], (s_new, t.shape[1])) for hh in range(nh)], axis=0)

    @pl.when(p == 0)
    def _():
        m_ref[...] = jnp.full_like(m_ref, NEG_BIG)
        l_ref[...] = jnp.zeros_like(l_ref)
        acc_ref[...] = jnp.zeros_like(acc_ref)
        carry_ref[...] = jnp.zeros_like(carry_ref)
        cn_ref[...] = jnp.dot(lfn_ref[0], uincl_ref[...], preferred_element_type=F32,
                              precision=HIGHEST)

    q = qbd_ref[0]
    cnx = expand(cn_ref[...])
    lane = lax.broadcasted_iota(jnp.int32, (rows, PAGE), 1)
    tq = lax.broadcasted_iota(jnp.int32, (rows, PAGE), 0) % s_new
    cn_col = jnp.sum(jnp.where(lane == tq, cnx, 0.0), axis=1, keepdims=True)

    def attend(kts, vts, biases, keep):
        ss = [jnp.dot(q, kt, preferred_element_type=F32) + b for kt, b in zip(kts, biases)]
        if keep is not None:
            ss = [jnp.where(keep, s, -jnp.inf) for s in ss]
        smax = functools.reduce(jnp.maximum, ss)
        m_prev = m_ref[...]
        m_new = jnp.maximum(m_prev, jnp.max(smax, axis=1, keepdims=True))
        alpha = jnp.exp(m_prev - m_new)
        prs = [jnp.exp(s - m_new) for s in ss]
        psum = functools.reduce(jnp.add, prs)
        l_ref[...] = alpha * l_ref[...] + jnp.sum(psum, axis=1, keepdims=True)
        pvs = [lax.dot_general(pr.astype(BF16), vt, (((1,), (1,)), ((), ())),
                               preferred_element_type=F32) for pr, vt in zip(prs, vts)]
        acc_ref[...] = alpha * acc_ref[...] + functools.reduce(jnp.add, pvs)
        m_ref[...] = m_new

    carry = carry_ref[...]
    biases = []
    for g in range(g_pages):
        ds = ds_refs[g][0]
        biases.append(expand(ds[:, :PAGE] + carry) + cn_col)
        carry = carry + ds[:, PAGE:PAGE + 1]
    carry_ref[...] = carry
    attend([r[...].astype(BF16) for r in kc_refs], [r[...].astype(BF16) for r in vc_refs],
           biases, None)

    @pl.when(p == n_steps - 1)
    def _():
        attend([kn_ref[0]], [vn_ref[0]], [cn_col - cnx], lane <= tq)
        out = acc_ref[...] / l_ref[...]
        width = out.shape[1]
        r = lax.broadcasted_iota(jnp.int32, (rows, width), 0) // s_new
        cidx = lax.broadcasted_iota(jnp.int32, (rows, width), 1) // HEAD_DIM
        out = jnp.where(r == cidx, out, 0.0)
        o_ref[0] = jnp.sum(out.reshape(nh, s_new, width), axis=0).astype(o_ref.dtype)


def _decode(pt_flat, qbd, kn_pad, vn_pad, lfn_t, kc, vc, dsuf, uincl, n_pages):
    bd_, rows, db = qbd.shape
    nh = dsuf.shape[1]
    s_new = rows // nh
    g_pages = PAGES_PER_STEP
    n_steps = n_pages // g_pages
    seqb = lambda b, p, pt: (b, 0, 0)

    def page(g):
        return lambda b, p, pt: (pt[b * n_pages + (n_pages - 1 - (p * g_pages + g))], 0, 0)

    in_specs = [
        pl.BlockSpec((1, rows, db), seqb),
        pl.BlockSpec((1, db, PAGE), seqb),
        pl.BlockSpec((1, db, PAGE), seqb),
        pl.BlockSpec((1, nh, PAGE), seqb),
        pl.BlockSpec((PAGE, PAGE), lambda b, p, pt: (0, 0)),
    ]
    in_specs += [pl.BlockSpec((None, db, PAGE), page(g)) for g in range(g_pages)]
    in_specs += [pl.BlockSpec((None, db, PAGE), page(g)) for g in range(g_pages)]
    in_specs += [pl.BlockSpec((1, nh, 2 * PAGE), page(g)) for g in range(g_pages)]
    grid_spec = pltpu.PrefetchScalarGridSpec(
        num_scalar_prefetch=1, grid=(bd_, n_steps), in_specs=in_specs,
        out_specs=pl.BlockSpec((1, s_new, db), seqb),
        scratch_shapes=[pltpu.VMEM((rows, 1), F32), pltpu.VMEM((rows, 1), F32),
                        pltpu.VMEM((rows, db), F32), pltpu.VMEM((nh, 1), F32),
                        pltpu.VMEM((nh, PAGE), F32)],
    )
    return pl.pallas_call(
        functools.partial(_decode_kernel, n_steps), grid_spec=grid_spec,
        out_shape=jax.ShapeDtypeStruct((bd_, s_new, db), BF16),
        compiler_params=_cparams(("arbitrary", "arbitrary")), name="fox_decode",
    )(pt_flat, qbd, kn_pad, vn_pad, lfn_t, uincl, *([kc] * g_pages), *([vc] * g_pages),
      *([dsuf] * g_pages))


def _outproj_kernel(n_prompt_tiles, x_ref, oa_ref, obp_ref, obs_ref, woa_ref, wob_ref, g2_ref,
                    wr_ref, br_ref, x1_ref, h2_ref, ti_ref, tg_ref):
    i = pl.program_id(0)
    ob = jnp.where(i >= n_prompt_tiles, obs_ref[...], obp_ref[...])
    mix = (jnp.dot(oa_ref[...], woa_ref[...], preferred_element_type=F32)
           + jnp.dot(ob, wob_ref[...], preferred_element_type=F32))
    x1 = x_ref[...] + mix
    x1_ref[...] = x1
    ms = jnp.mean(x1 * x1, axis=-1, keepdims=True)
    h2 = x1 * lax.rsqrt(ms + EPS) * g2_ref[...]
    h2_ref[...] = h2
    logits = jnp.dot(h2, wr_ref[...], preferred_element_type=F32, precision=HIGHEST) + br_ref[...]
    lane = lax.broadcasted_iota(jnp.int32, logits.shape, 1)
    cur = jnp.where(lane < N_EXPERTS, logits, -jnp.inf)
    idx_out = jnp.zeros(logits.shape, jnp.int32)
    val_out = jnp.zeros(logits.shape, F32)
    v0 = None
    denom = None
    for k in range(TOP_K):
        mval = jnp.max(cur, axis=1, keepdims=True)
        midx = jnp.min(jnp.where(cur == mval, lane, LANES), axis=1, keepdims=True)
        if k == 0:
            v0 = mval
            ek = jnp.ones_like(mval)
            denom = ek
        else:
            ek = jnp.exp(mval - v0)
            denom = denom + ek
        idx_out = jnp.where(lane == k, midx, idx_out)
        val_out = jnp.where(lane == k, ek, val_out)
        cur = jnp.where(lane == midx, -jnp.inf, cur)
    ti_ref[...] = idx_out
    tg_ref[...] = val_out / denom


def _outproj(x, oa, obp, obs, woa, wob, g2, wr, br, n_prompt_tiles):
    nt, d = x.shape
    half = oa.shape[1]
    row = lambda w: pl.BlockSpec((TM, w), lambda i: (i, 0))
    in_specs = [
        row(d), row(half),
        pl.BlockSpec((TM, half), lambda i: (jnp.minimum(i, n_prompt_tiles - 1), 0)),
        _full((TM, half)),
        _full((half, d)), _full((half, d)), _full((1, d)), _full((d, LANES)), _full((1, LANES)),
    ]
    out_shape = [jax.ShapeDtypeStruct((nt, d), F32), jax.ShapeDtypeStruct((nt, d), F32),
                 jax.ShapeDtypeStruct((nt, LANES), jnp.int32), jax.ShapeDtypeStruct((nt, LANES), F32)]
    return pl.pallas_call(
        functools.partial(_outproj_kernel, n_prompt_tiles),
        grid=(nt // TM,), in_specs=in_specs,
        out_specs=[row(d), row(d), row(LANES), row(LANES)], out_shape=out_shape,
        compiler_params=_cparams(("arbitrary",)), name="outproj_router",
    )(x, oa, obp, obs, woa, wob, g2, wr, br)


def _pack_pairs(x):
    n = x.shape[1] // 2
    lo = pltpu.bitcast(x[:, :n], jnp.uint32) >> 16
    hi = pltpu.bitcast(x[:, n:], jnp.uint32) & jnp.uint32(0xFFFF0000)
    return lo | hi


def _unpack_pairs(w):
    a = pltpu.bitcast(w << 16, F32)
    b = pltpu.bitcast(w & jnp.uint32(0xFFFF0000), F32)
    return jnp.concatenate([a, b], axis=1).astype(BF16)


def _dispatch_kernel(h2_ref, ti_ref, tg_ref, lstrict_ref, ustrict_ref, ones_ref,
                     xl_ref, pg_ref, cnt_ref):
    rows = h2_ref.shape[0]
    rp = xl_ref.shape[0]
    half = h2_ref.shape[1] // 2
    lane = lax.broadcasted_iota(jnp.int32, (rows, LANES), 1)
    ti = ti_ref[...]
    tg = tg_ref[...]
    chose = [lane == ti[:, k:k + 1] for k in range(TOP_K)]
    member = functools.reduce(jnp.logical_or, chose)
    e_mat = jnp.where(member, 1.0, 0.0)
    rank = jnp.dot(lstrict_ref[...], e_mat.astype(BF16), preferred_element_type=F32)
    cnt = rank[rows - 1:, :] + e_mat[rows - 1:, :]
    gran = jnp.floor((cnt + (GRANULE - 1)) * (1.0 / GRANULE))
    off = jnp.dot(jnp.broadcast_to(gran * GRANULE, (8, LANES)).astype(BF16), ustrict_ref[...],
                  preferred_element_type=F32)[:1, :]
    pos = off + rank
    pk = pltpu.roll(tg, TOP_K, axis=1)
    for k in range(TOP_K):
        pos_k = jnp.sum(jnp.where(chose[k], pos, 0.0), axis=1, keepdims=True)
        pk = jnp.where(lane == k, pos_k, pk)
    pg_ref[...] = pk
    cnt_ref[...] = jnp.broadcast_to(gran, (8, LANES))

    pkt = pk.T
    r = lax.broadcasted_iota(jnp.int32, (rp, rows), 0).astype(F32)
    sel = jnp.zeros((rp, rows), F32)
    gsel = jnp.zeros((rp, rows), F32)
    for k in range(TOP_K):
        hit = r == pkt[k:k + 1, :]
        sel = jnp.where(hit, 1.0, sel)
        gsel = jnp.where(hit, pkt[TOP_K + k:TOP_K + k + 1, :], gsel)
    xl = jnp.dot(sel.astype(BF16), h2_ref[...].astype(BF16), preferred_element_type=F32)
    g_hi = gsel.astype(BF16)
    g_lo = (gsel - g_hi.astype(F32)).astype(BF16)
    gate = (jnp.dot(g_hi, ones_ref[...], preferred_element_type=F32)
            + jnp.dot(g_lo, ones_ref[...], preferred_element_type=F32))
    xl_ref[:, :half] = _pack_pairs(xl)
    xl_ref[:, half:] = pltpu.bitcast(gate, jnp.uint32)


def _dispatch(h2, ti, tg, lstrict, ustrict, ones):
    nt, d = h2.shape
    n_tiles = nt // TM
    rp = TOP_K * TM + N_EXPERTS * GRANULE
    row = lambda w: pl.BlockSpec((TM, w), lambda i: (i, 0))
    return pl.pallas_call(
        _dispatch_kernel, grid=(n_tiles,),
        in_specs=[row(d), row(LANES), row(LANES), _full((TM, TM)), _full((LANES, LANES)),
                  _full((TM, LANES))],
        out_specs=[pl.BlockSpec((rp, d // 2 + LANES), lambda i: (i, 0)), row(LANES),
                   pl.BlockSpec((8, LANES), lambda i: (i, 0))],
        out_shape=[jax.ShapeDtypeStruct((n_tiles * rp, d // 2 + LANES), jnp.uint32),
                   jax.ShapeDtypeStruct((nt, LANES), F32),
                   jax.ShapeDtypeStruct((n_tiles * 8, LANES), F32)],
        compiler_params=_cparams(("arbitrary",)), name="dispatch",
    )(h2, ti, tg, lstrict, ustrict, ones)


def _granule_copies(idx_ref, base, hbm, buf, sem, to_hbm):
    copies = []
    for q in range(MOE_BLOCK // GRANULE):
        g = idx_ref[base + q]
        hrows = hbm.at[pl.ds(pl.multiple_of(g * GRANULE, GRANULE), GRANULE)]
        brows = buf.at[pl.ds(q * GRANULE, GRANULE)]
        copies.append(pltpu.make_async_copy(brows, hrows, sem) if to_hbm
                      else pltpu.make_async_copy(hrows, brows, sem))
    return copies


def _expert_kernel(be_ref, bv_ref, gsrc_ref, gdst_ref, xl_hbm, wgu_ref, bgu_ref, wdn_ref, bdn_ref,
                   yz_hbm, yl_hbm, xs_buf, y_buf, wgu_bf, wdn_bf, sem_in, sem_out):
    del yz_hbm
    i = pl.program_id(0)
    n_blocks = pl.num_programs(0)
    gpb = MOE_BLOCK // GRANULE
    d_ff = wdn_ref.shape[1]
    half = y_buf.shape[1]
    slot = i % 2

    def wait_in(s):
        pltpu.make_async_copy(xl_hbm.at[pl.ds(0, MOE_BLOCK)], xs_buf.at[s], sem_in.at[s]).wait()

    def wait_out():
        pltpu.make_async_copy(y_buf, yl_hbm.at[pl.ds(0, MOE_BLOCK)], sem_out.at[0]).wait()

    @pl.when((i == 0) & (bv_ref[0] == 1))
    def _():
        for c in _granule_copies(gsrc_ref, 0, xl_hbm, xs_buf.at[0], sem_in.at[0], False):
            c.start()

    @pl.when(bv_ref[i] == 1)
    def _():
        nxt = jnp.minimum(i + 1, n_blocks - 1)
        has_next = (i + 1 < n_blocks) & (bv_ref[nxt] == 1)

        @pl.when(has_next)
        def _():
            for c in _granule_copies(gsrc_ref, nxt * gpb, xl_hbm, xs_buf.at[1 - slot],
                                     sem_in.at[1 - slot], False):
                c.start()

        prev = be_ref[jnp.maximum(i - 1, 0)]

        @pl.when((i == 0) | (prev != be_ref[i]))
        def _():
            wgu_bf[...] = wgu_ref[0].astype(BF16)
            wdn_bf[...] = wdn_ref[0].astype(BF16)

        wait_in(slot)
        xs = _unpack_pairs(xs_buf[slot, :, :half])
        gate = pltpu.bitcast(xs_buf[slot, :, half:], F32)
        gu = jnp.dot(xs, wgu_bf[...], preferred_element_type=F32) + bgu_ref[0]
        glu = jnp.minimum(gu[:, :d_ff], SWIGLU_LIMIT)
        lin = jnp.clip(gu[:, d_ff:], -SWIGLU_LIMIT, SWIGLU_LIMIT)
        act = glu * jax.nn.sigmoid(SWIGLU_ALPHA * glu) * (lin + 1.0)
        y = jnp.dot(act.astype(BF16), wdn_bf[...], preferred_element_type=F32) + bdn_ref[0]
        y = y * jnp.concatenate([gate] * (y.shape[1] // LANES), axis=1)
        packed = _pack_pairs(y.astype(BF16).astype(F32))

        @pl.when(i > 0)
        def _():
            wait_out()

        y_buf[...] = packed
        for c in _granule_copies(gdst_ref, i * gpb, yl_hbm, y_buf, sem_out.at[0], True):
            c.start()

        @pl.when(jnp.logical_not(has_next))
        def _():
            wait_out()


def _experts(block_e, block_valid, gsrc, gdst, xl, yl_zero, wgu, bgu, wdn, bdn):
    n_blocks = block_e.shape[0]
    _, d, two_ff = wgu.shape
    d_ff = two_ff // 2
    eidx = lambda i, be, bv, gs, gd: (be[i], 0, 0)
    grid_spec = pltpu.PrefetchScalarGridSpec(
        num_scalar_prefetch=4, grid=(n_blocks,),
        in_specs=[
            pl.BlockSpec(memory_space=pl.ANY),
            pl.BlockSpec((1, d, two_ff), eidx),
            pl.BlockSpec((1, 1, two_ff), eidx),
            pl.BlockSpec((1, d_ff, d), eidx),
            pl.BlockSpec((1, 1, d), eidx),
            pl.BlockSpec(memory_space=pl.ANY),
        ],
        out_specs=pl.BlockSpec(memory_space=pl.ANY),
        scratch_shapes=[pltpu.VMEM((2, MOE_BLOCK, xl.shape[1]), jnp.uint32),
                        pltpu.VMEM((MOE_BLOCK, d // 2), jnp.uint32),
                        pltpu.VMEM((d, two_ff), BF16), pltpu.VMEM((d_ff, d), BF16),
                        pltpu.SemaphoreType.DMA((2,)), pltpu.SemaphoreType.DMA((1,))],
    )
    return pl.pallas_call(
        _expert_kernel, grid_spec=grid_spec,
        out_shape=jax.ShapeDtypeStruct(yl_zero.shape, jnp.uint32),
        input_output_aliases={9: 0},
        compiler_params=_cparams(("arbitrary",)), name="experts",
    )(block_e, block_valid, gsrc, gdst, xl, wgu, bgu, wdn, bdn, yl_zero)


def _combine_kernel(x1_ref, pg_ref, yl_ref, o_ref):
    rows = x1_ref.shape[0]
    rp = yl_ref.shape[0]
    pg = pg_ref[...]
    r = lax.broadcasted_iota(jnp.int32, (rows, rp), 1).astype(F32)
    sel = jnp.zeros((rows, rp), F32)
    for k in range(TOP_K):
        sel = jnp.where(r == pg[:, k:k + 1], 1.0, sel)
    yl = _unpack_pairs(yl_ref[...])
    o_ref[...] = x1_ref[...] + jnp.dot(sel.astype(BF16), yl, preferred_element_type=F32)


def _combine(x1, pg, yl):
    nt, d = x1.shape
    n_tiles = nt // TM
    rp = TOP_K * TM + N_EXPERTS * GRANULE
    row = lambda w: pl.BlockSpec((TM, w), lambda i: (i, 0))
    return pl.pallas_call(
        _combine_kernel, grid=(n_tiles,),
        in_specs=[row(d), row(LANES), pl.BlockSpec((rp, d // 2), lambda i: (i, 0))],
        out_specs=row(d),
        out_shape=jax.ShapeDtypeStruct((nt, d), F32),
        compiler_params=_cparams(("arbitrary",)), name="combine",
    )(x1, pg, yl)


def _route(cnt_rows, n_tiles, nt):
    i32 = jnp.int32
    cg = cnt_rows.reshape(n_tiles, 8, LANES)[:, 0, :N_EXPERTS].astype(i32)
    gp_tile = (TOP_K * TM + N_EXPERTS * GRANULE) // GRANULE
    gpb = MOE_BLOCK // GRANULE
    seg = jnp.arange(n_tiles, dtype=i32)[:, None] * gp_tile + jnp.cumsum(cg, axis=1) - cg
    cum_incl = jnp.cumsum(cg, axis=0)
    cum_excl = cum_incl - cg
    ng = cum_incl[-1]
    nb = (ng + gpb - 1) // gpb
    bend = jnp.cumsum(nb)
    bstart = bend - nb
    max_gran = (nt * TOP_K) // GRANULE + n_tiles * N_EXPERTS
    n_blocks = -(-max_gran // gpb) + N_EXPERTS
    bi = jnp.arange(n_blocks, dtype=i32)
    block_e = jnp.minimum(jnp.sum((bend[None, :] <= bi[:, None]).astype(i32), axis=1),
                          N_EXPERTS - 1).astype(i32)
    block_valid = (bi < bend[-1]).astype(i32)
    e_s = jnp.repeat(block_e, gpb)
    q = jnp.tile(jnp.arange(gpb, dtype=i32), n_blocks)
    u = (jnp.repeat(bi, gpb) - bstart[e_s]) * gpb + q
    valid = (u < ng[e_s]) & (jnp.repeat(block_valid, gpb) == 1)
    tile_of = jnp.minimum(jnp.sum((cum_incl.T[e_s] <= u[:, None]).astype(i32), axis=1), n_tiles - 1)
    flat = tile_of * N_EXPERTS + e_s
    g = seg.reshape(-1)[flat] + u - cum_excl.reshape(-1)[flat]
    gsrc = jnp.where(valid, g, 0).astype(i32)
    gdst = jnp.where(valid, g, n_tiles * gp_tile + q).astype(i32)
    return block_e, block_valid, gsrc, gdst


def _bias_selector(nh, db):
    rows = jnp.arange(LANES)[:, None]
    cols = jnp.arange(2 * db)[None, :]
    head = cols // LANES
    off = cols % LANES - BIAS_LANE
    part = rows // nh
    is_split = (part < N_SPLIT) & (rows % nh == head)
    is_one = rows == N_SPLIT * nh
    q_sel = ((is_split & (off == part)) | (is_one & (off >= N_SPLIT) & (off < 2 * N_SPLIT)))
    k_sel = (is_one & (off >= 0) & (off < N_SPLIT)).astype(F32) \
        - (is_split & (off == part + N_SPLIT)).astype(F32)
    return jnp.concatenate([q_sel.astype(F32), k_sel], axis=1).astype(BF16)


def kernel(x_prompt, x_sample, cache_k, cache_v, cache_logf, page_table, norm1_g, w_in, b_f, gv_g,
           w_s, b_s, q_g, k_g, w_out, norm2_g, w_router, b_router, w_gu, b_gu, w_dn, b_dn):
    batch, seq, d = x_prompt.shape
    dec_b, dec_s, _ = x_sample.shape
    depth = w_in.shape[0]
    ha = gv_g.shape[1]
    hb = b_f.shape[1]
    da, db = ha * HEAD_DIM, hb * HEAD_DIM
    n_main = 2 * da + 3 * db
    n_phys = cache_k.shape[1]
    n_pages = page_table.shape[1]
    t_prompt = batch * seq
    t_dec = dec_b * dec_s
    assert t_dec == TM and seq % TM == 0 and TM % CHUNK == 0 and seq % BQ == 0 and BQ == BK
    assert da % LANES == 0 and db % LANES == 0 and dec_s <= CHUNK and TM % dec_s == 0
    assert cache_k.shape[2] == PAGE and n_pages % PAGES_PER_STEP == 0
    assert (N_SPLIT + 1) * hb <= LANES
    n_prompt_tiles = t_prompt // TM
    tiles_per_seq = seq // TM

    x = jnp.concatenate([x_prompt.reshape(t_prompt, d), x_sample.reshape(t_dec, d)], axis=0)

    wm_all = w_in[:, :, :n_main].astype(BF16)
    wf_all = jnp.pad(w_in[:, :, n_main:], ((0, 0), (0, 0), (0, LANES - hb))).astype(BF16)
    bf_all = jnp.pad(b_f, ((0, 0), (0, LANES - hb))).reshape(depth, 1, LANES)
    gv_all = gv_g.reshape(depth, 1, da)
    qg_all = jnp.tile(q_g, (1, hb)).reshape(depth, 1, db)
    kg_all = jnp.tile(k_g, (1, hb)).reshape(depth, 1, db)
    head_of = jnp.arange(db) // HEAD_DIM
    bd = (head_of[:, None] == head_of[None, :]).astype(BF16)
    tril = jnp.tril(jnp.ones((CHUNK, CHUNK), F32))
    wt = w_s * tril
    eye_p = jnp.eye(TM // CHUNK, dtype=F32)
    eye_s = jnp.eye(TM // dec_s, dtype=F32)
    wblk_p = jnp.einsum('ab,lhts->lhatbs', eye_p, wt).reshape(depth, ha, TM, TM)
    wblk_s = jnp.einsum('ab,lhts->lhatbs', eye_s, wt[:, :, :dec_s, :dec_s]).reshape(depth, ha, TM, TM)
    wblk_all = jnp.stack([wblk_p, wblk_s], axis=1).astype(BF16)
    bs_t = jnp.repeat(jnp.swapaxes(b_s, 1, 2), HEAD_DIM, axis=2)
    bias_p = jnp.tile(bs_t, (1, TM // CHUNK, 1))
    bias_s = jnp.tile(bs_t[:, :dec_s], (1, TM // dec_s, 1))
    bias_all = jnp.stack([bias_p, bias_s], axis=1)
    ltri = jnp.tril(jnp.ones((TM, TM), F32))
    sel = _bias_selector(hb, db)
    woa_all = w_out[:, :da].astype(BF16)
    wob_all = w_out[:, da:].astype(BF16)
    wr_all = jnp.pad(w_router, ((0, 0), (0, 0), (0, LANES - N_EXPERTS)))
    br_all = jnp.pad(b_router, ((0, 0), (0, LANES - N_EXPERTS))).reshape(depth, 1, LANES)
    two_ff = w_gu.shape[3]
    wgu_all = w_gu.reshape(depth * N_EXPERTS, d, two_ff)
    bgu_all = b_gu.reshape(depth * N_EXPERTS, 1, two_ff)
    wdn_all = w_dn.reshape(depth * N_EXPERTS, two_ff // 2, d)
    bdn_all = b_dn.reshape(depth * N_EXPERTS, 1, d)
    nt = t_prompt + t_dec
    n_tiles = nt // TM
    ar_t = jnp.arange(TM)
    lstrict = (ar_t[:, None] > ar_t[None, :]).astype(BF16)
    ar_l = jnp.arange(LANES)
    ustrict_e = (ar_l[:, None] < ar_l[None, :]).astype(BF16)
    ones_tm = jnp.ones((TM, LANES), BF16)
    sorted_rows = n_tiles * (TOP_K * TM + N_EXPERTS * GRANULE)
    yl_zero = jnp.zeros((sorted_rows + MOE_BLOCK, d // 2), jnp.uint32)

    kc = jnp.transpose(cache_k, (0, 1, 3, 4, 2)).reshape(depth * n_phys, db, PAGE)
    vc = jnp.transpose(cache_v, (0, 1, 3, 4, 2)).reshape(depth * n_phys, db, PAGE)
    lfc_rows = jnp.swapaxes(cache_logf, 2, 3).reshape(depth * n_phys * hb, PAGE)
    pt = page_table.reshape(dec_b * n_pages).astype(jnp.int32)
    ar = jnp.arange(PAGE)
    ustrict = (ar[:, None] > ar[None, :]).astype(F32)
    uincl = (ar[:, None] <= ar[None, :]).astype(F32)
    u2 = jnp.concatenate([ustrict, jnp.ones((PAGE, PAGE), F32)], axis=1)
    dsuf = _suffix(lfc_rows, u2).reshape(depth * n_phys, hb, 2 * PAGE)
    eye_h = jnp.eye(hb, dtype=F32)

    kp_l, vp_l, fp_l, ks_l, vs_l, fs_l, cv_l = [], [], [], [], [], [], []
    for l in range(depth):
        oa, qa, kt, va, k32, v32, logf, va_s, q_s = _project(
            x, norm1_g[l].reshape(1, d), wm_all[l], wf_all[l], bf_all[l], gv_all[l], qg_all[l],
            kg_all[l], bd, wblk_all[l], bias_all[l], ltri, sel, n_prompt_tiles, tiles_per_seq)

        ob_p = _flash(qa, kt, va, batch, seq)

        q4 = q_s.reshape(dec_b, dec_s, hb, HEAD_DIM) * (HEAD_DIM ** -0.5)
        qbd = jnp.einsum('bthd,hg->bhtgd', q4, eye_h).reshape(dec_b, hb * dec_s, db).astype(BF16)
        k_new = k32[t_prompt:].reshape(dec_b, dec_s, db)
        v_new = v32[t_prompt:].reshape(dec_b, dec_s, db)
        lf_new = logf[t_prompt:].reshape(dec_b, dec_s, hb)
        padr = ((0, 0), (0, 0), (0, PAGE - dec_s))
        kn_pad = jnp.pad(jnp.swapaxes(k_new, 1, 2), padr).astype(BF16)
        vn_pad = jnp.pad(jnp.swapaxes(v_new, 1, 2), padr).astype(BF16)
        lfn_t = jnp.pad(jnp.swapaxes(lf_new, 1, 2), ((0, 0), (0, 0), (0, PAGE - dec_s)))
        ob_s = _decode(pt + l * n_phys, qbd, kn_pad, vn_pad, lfn_t, kc, vc, dsuf, uincl,
                       n_pages).reshape(t_dec, db)

        x1, h2, ti, tg = _outproj(x, oa, ob_p, ob_s, woa_all[l], wob_all[l],
                                  norm2_g[l].reshape(1, d), wr_all[l], br_all[l], n_prompt_tiles)
        xl, pg, cnt_rows = _dispatch(h2, ti, tg, lstrict, ustrict_e, ones_tm)
        block_e, block_valid, gsrc, gdst = _route(cnt_rows, n_tiles, nt)
        yl = _experts(block_e + l * N_EXPERTS, block_valid, gsrc, gdst, xl, yl_zero, wgu_all,
                      bgu_all, wdn_all, bdn_all)
        x = _combine(x1, pg, yl)

        kp_l.append(k32[:t_prompt].reshape(batch, seq, hb, HEAD_DIM))
        vp_l.append(v32[:t_prompt].reshape(batch, seq, hb, HEAD_DIM))
        fp_l.append(logf[:t_prompt].reshape(batch, seq, hb))
        ks_l.append(k_new.reshape(dec_b, dec_s, hb, HEAD_DIM))
        vs_l.append(v_new.reshape(dec_b, dec_s, hb, HEAD_DIM))
        fs_l.append(lf_new)
        cv_l.append(va_s.reshape(dec_b, dec_s, ha, HEAD_DIM))

    y_prompt = x[:t_prompt].reshape(batch, seq, d)
    y_sample = x[t_prompt:].reshape(dec_b, dec_s, d)
    return (y_prompt, y_sample, jnp.stack(kp_l), jnp.stack(vp_l), jnp.stack(fp_l),
            jnp.stack(ks_l), jnp.stack(vs_l), jnp.stack(fs_l), jnp.stack(cv_l))
```

```python
import functools
import math

import jax
import jax.numpy as jnp
from jax import lax
from jax.experimental import pallas as pl
from jax.experimental.pallas import tpu as pltpu

F32 = jnp.float32
BF16 = jnp.bfloat16
HIGHEST = lax.Precision.HIGHEST

HEAD_DIM = 64
CHUNK = 128
PAGE = 128
N_EXPERTS = 32
TOP_K = 4
EPS = 1e-6
SWIGLU_ALPHA = 1.702
SWIGLU_LIMIT = 7.0
LOG2E = math.log2(math.e)
NEG_BIG = -1e30

LANES = 128
TM = 256
BQ = 512
BK = 512
QC = 128
KC = 256
PAGES_PER_STEP = 16
SUFFIX_ROWS = 4096
MOE_BLOCK = 256
GRANULE = 8
VMEM_LIMIT = 56 * 1024 * 1024

BIAS_LANE = HEAD_DIM
N_SPLIT = 3


def _cparams(sem):
    return pltpu.CompilerParams(dimension_semantics=sem, vmem_limit_bytes=VMEM_LIMIT)


def _full(shape):
    nd = len(shape)
    return pl.BlockSpec(shape, lambda *_: (0,) * nd)


def _proj_kernel(tiles_per_seq, n_prompt_tiles,
                 x_ref, g1_ref, wm_ref, wf_ref, bf_ref, gv_ref, qg_ref, kg_ref, bd_ref,
                 wblk_ref, bias_ref, ltri_ref, sel_ref, kin_ref, vin_ref, lin_ref,
                 oa_ref, qa_ref, kt_ref, va_ref, kst_ref, vst_ref, lst_ref,
                 vn_ref, qs_ref, ks_ref, vs_ref, ls_ref, carry_ref):
    del kin_ref, vin_ref, lin_ref
    i = pl.program_id(0)
    da = gv_ref.shape[1]
    db = qg_ref.shape[1]
    x = x_ref[...]
    ms = jnp.mean(x * x, axis=-1, keepdims=True)
    h = (x * lax.rsqrt(ms + EPS) * g1_ref[...]).astype(BF16)
    p = jnp.dot(h, wm_ref[...], preferred_element_type=F32)
    fl = jnp.dot(h, wf_ref[...], preferred_element_type=F32)

    def group_norm(t, gamma):
        sq = (t * t).astype(BF16)
        gms = jnp.dot(sq, bd_ref[...], preferred_element_type=F32) * (1.0 / HEAD_DIM)
        return t * lax.rsqrt(gms + EPS) * gamma

    u = jax.nn.gelu(p[:, 0:da])
    van = group_norm(jax.nn.gelu(p[:, da:2 * da]), gv_ref[...])
    qn = group_norm(p[:, 2 * da:2 * da + db], qg_ref[...])
    kn = group_norm(p[:, 2 * da + db:2 * da + 2 * db], kg_ref[...])
    vv = p[:, 2 * da + 2 * db:2 * da + 3 * db]

    vn_ref[...] = van
    qs_ref[...] = qn
    ks_ref[...] = kn
    vs_ref[...] = vv

    lane = lax.broadcasted_iota(jnp.int32, (x.shape[0], LANES), 1)
    low = lane < HEAD_DIM

    van_bf = van.astype(BF16)
    for j in range(da // LANES):
        rhs = van_bf[:, j * LANES:(j + 1) * LANES]
        z0 = jnp.dot(wblk_ref[0, 2 * j], rhs, preferred_element_type=F32)
        z1 = jnp.dot(wblk_ref[0, 2 * j + 1], rhs, preferred_element_type=F32)
        z = jnp.where(low, z0, z1) + bias_ref[0, :, j * LANES:(j + 1) * LANES]
        oa_ref[:, j * LANES:(j + 1) * LANES] = (u[:, j * LANES:(j + 1) * LANES] * z).astype(BF16)

    nh = ls_ref.shape[1]
    z = fl + bf_ref[...]
    lf = jnp.minimum(z, 0.0) - jnp.log1p(jnp.exp(-jnp.abs(z)))
    lf = jnp.where(lane < nh, lf, 0.0)
    ls_ref[...] = lf[:, :nh]

    @pl.when(i % tiles_per_seq == 0)
    def _():
        carry_ref[...] = jnp.zeros_like(carry_ref)

    c = jnp.dot(ltri_ref[...], lf, preferred_element_type=F32, precision=HIGHEST) + carry_ref[...]
    carry_ref[...] = c[c.shape[0] - 1:, :]

    c2 = c * LOG2E
    hi = c2.astype(BF16).astype(F32)
    r1 = c2 - hi
    mid = r1.astype(BF16).astype(F32)
    lo = (r1 - mid).astype(BF16).astype(F32)
    c3 = (hi + pltpu.roll(mid, nh, axis=1) + pltpu.roll(lo, 2 * nh, axis=1)
          + jnp.where(lane == N_SPLIT * nh, 1.0, 0.0)).astype(BF16)
    extra = jnp.dot(c3, sel_ref[...], preferred_element_type=F32)

    qscale = (HEAD_DIM ** -0.5) * LOG2E
    kparts = []
    for j in range(db // LANES):
        qp = qn[:, j * LANES:(j + 1) * LANES] * qscale
        kp = kn[:, j * LANES:(j + 1) * LANES]
        vp = vv[:, j * LANES:(j + 1) * LANES]
        for half, (qh, kh, vh) in enumerate(((qp, kp, vp),
                                             (pltpu.roll(qp, HEAD_DIM, axis=1),
                                              pltpu.roll(kp, HEAD_DIM, axis=1),
                                              pltpu.roll(vp, HEAD_DIM, axis=1)))):
            hh = 2 * j + half
            cols = slice(hh * LANES, (hh + 1) * LANES)
            kcols = slice(2 * db + hh * LANES, 2 * db + (hh + 1) * LANES)
            qa_ref[:, cols] = jnp.where(low, qh, extra[:, cols]).astype(BF16)
            va_ref[:, cols] = jnp.where(low, vh, 1.0).astype(BF16)
            kparts.append(jnp.where(low, kh, extra[:, kcols]))
    ktf = jnp.concatenate(kparts, axis=1).T
    kt_ref[...] = ktf.astype(BF16)

    @pl.when(i < n_prompt_tiles)
    def _():
        kst_ref[...] = jnp.concatenate(
            [ktf[hh * LANES:hh * LANES + HEAD_DIM, :] for hh in range(db // HEAD_DIM)], axis=0)
        vst_ref[...] = vv.T
        lst_ref[...] = lf.T[:nh, :]


def _project(layer, x, g1, wm, wf, bfp, gv, qg, kg, bd, wblk, bias, ltri, sel, kstack, vstack, lstack,
             n_prompt_tiles, tiles_per_seq):
    nt, d = x.shape
    n_main = wm.shape[1]
    da, db = gv.shape[1], qg.shape[1]
    nh = db // HEAD_DIM
    grid = (nt // TM,)
    row = lambda w: pl.BlockSpec((TM, w), lambda i: (i, 0))
    variant = lambda i: i // n_prompt_tiles
    any_spec = pl.BlockSpec(memory_space=pl.ANY)

    def seq_block(i):
        ip = jnp.minimum(i, n_prompt_tiles - 1)
        return (layer, ip // tiles_per_seq, 0, ip % tiles_per_seq)

    in_specs = [
        row(d), _full((1, d)), _full((d, n_main)), _full((d, LANES)), _full((1, LANES)),
        _full((1, da)), _full((1, db)), _full((1, db)), _full((db, db)),
        pl.BlockSpec((1, da // HEAD_DIM, TM, TM), lambda i: (variant(i), 0, 0, 0)),
        pl.BlockSpec((1, TM, da), lambda i: (variant(i), 0, 0)),
        _full((TM, TM)), _full((LANES, 4 * db)), any_spec, any_spec, any_spec,
    ]
    out_shape = [
        jax.ShapeDtypeStruct((nt, da), BF16),
        jax.ShapeDtypeStruct((nt, 2 * db), BF16),
        jax.ShapeDtypeStruct((2 * db, nt), BF16),
        jax.ShapeDtypeStruct((nt, 2 * db), BF16),
        jax.ShapeDtypeStruct(kstack.shape, F32),
        jax.ShapeDtypeStruct(vstack.shape, F32),
        jax.ShapeDtypeStruct(lstack.shape, F32),
        jax.ShapeDtypeStruct((TM, da), F32),
        jax.ShapeDtypeStruct((TM, db), F32),
        jax.ShapeDtypeStruct((TM, db), F32),
        jax.ShapeDtypeStruct((TM, db), F32),
        jax.ShapeDtypeStruct((TM, nh), F32),
    ]
    out_specs = [row(da), row(2 * db), pl.BlockSpec((2 * db, TM), lambda i: (0, i)), row(2 * db),
                 pl.BlockSpec((None, None, db, TM), seq_block),
                 pl.BlockSpec((None, None, db, TM), seq_block),
                 pl.BlockSpec((None, None, nh, TM), seq_block),
                 _full((TM, da)), _full((TM, db)), _full((TM, db)), _full((TM, db)), _full((TM, nh))]
    return pl.pallas_call(
        functools.partial(_proj_kernel, tiles_per_seq, n_prompt_tiles),
        grid=grid, in_specs=in_specs, out_specs=out_specs, out_shape=out_shape,
        scratch_shapes=[pltpu.VMEM((1, LANES), F32)],
        input_output_aliases={13: 4, 14: 5, 15: 6},
        compiler_params=_cparams(("arbitrary",)), name="proj",
    )(x, g1, wm, wf, bfp, gv, qg, kg, bd, wblk, bias, ltri, sel, kstack, vstack, lstack)


def _flash_kernel(qa_ref, kt_ref, va_ref, o_ref, m_ref, acc_ref):
    qi = pl.program_id(1)
    ki = pl.program_id(2)
    nh = va_ref.shape[1] // LANES

    @pl.when(ki == 0)
    def _():
        m_ref[...] = jnp.full_like(m_ref, NEG_BIG)
        acc_ref[...] = jnp.zeros_like(acc_ref)

    def step(diag):
        for hh in range(nh):
            for qc in range(BQ // QC):
                rows = slice(qc * QC, (qc + 1) * QC)
                q = qa_ref[rows, hh * LANES:(hh + 1) * LANES]
                m = m_ref[hh, rows, :]
                acc = acc_ref[hh, rows, :]
                for kc in range(BK // KC):
                    if diag and kc * KC > qc * QC + QC - 1:
                        continue
                    cols = slice(kc * KC, (kc + 1) * KC)
                    s = jnp.dot(q, kt_ref[hh * LANES:(hh + 1) * LANES, cols],
                                preferred_element_type=F32)
                    if diag and kc * KC + KC - 1 > qc * QC:
                        r = lax.broadcasted_iota(jnp.int32, (QC, KC), 0) + qc * QC
                        c = lax.broadcasted_iota(jnp.int32, (QC, KC), 1) + kc * KC
                        s = jnp.where(c <= r, s, -jnp.inf)
                    smax = jnp.maximum(s[:, :LANES], s[:, LANES:])
                    m_new = jnp.maximum(m, jnp.max(smax, axis=1, keepdims=True))
                    alpha = jnp.exp2(m - m_new)
                    pr = jnp.concatenate([jnp.exp2(s[:, :LANES] - m_new),
                                          jnp.exp2(s[:, LANES:] - m_new)], axis=1)
                    acc = alpha * acc + jnp.dot(pr.astype(BF16),
                                                va_ref[cols, hh * LANES:(hh + 1) * LANES],
                                                preferred_element_type=F32)
                    m = m_new
                m_ref[hh, rows, :] = m
                acc_ref[hh, rows, :] = acc

    @pl.when(ki < qi)
    def _():
        step(False)

    @pl.when(ki == qi)
    def _():
        step(True)
        lane = lax.broadcasted_iota(jnp.int32, (BQ, LANES), 1)
        for j in range(nh // 2):
            a0 = acc_ref[2 * j]
            a1 = acc_ref[2 * j + 1]
            num = jnp.where(lane < HEAD_DIM, a0, pltpu.roll(a1, HEAD_DIM, axis=1))
            den = jnp.where(lane < HEAD_DIM, pltpu.roll(a0, HEAD_DIM, axis=1), a1)
            o_ref[:, j * LANES:(j + 1) * LANES] = (num / den).astype(o_ref.dtype)


def _flash(qa, kt, va, batch, seq):
    wide = qa.shape[1]
    db = wide // 2
    nh = db // HEAD_DIM
    nq = seq // BQ
    nk = seq // BK
    qrow = lambda b, qi, ki: (b * nq + qi, 0)
    in_specs = [
        pl.BlockSpec((BQ, wide), qrow),
        pl.BlockSpec((wide, BK), lambda b, qi, ki: (0, b * nk + jnp.minimum(ki, qi))),
        pl.BlockSpec((BK, wide), lambda b, qi, ki: (b * nk + jnp.minimum(ki, qi), 0)),
    ]
    return pl.pallas_call(
        _flash_kernel,
        grid=(batch, nq, nk), in_specs=in_specs,
        out_specs=pl.BlockSpec((BQ, db), qrow),
        out_shape=jax.ShapeDtypeStruct((batch * seq, db), BF16),
        scratch_shapes=[pltpu.VMEM((nh, BQ, LANES), F32), pltpu.VMEM((nh, BQ, LANES), F32)],
        compiler_params=_cparams(("arbitrary", "arbitrary", "arbitrary")), name="fox_prompt",
    )(qa, kt, va)


def _suffix_kernel(lf_ref, u2_ref, o_ref):
    o_ref[...] = jnp.dot(lf_ref[...], u2_ref[...], preferred_element_type=F32, precision=HIGHEST)


def _suffix(lf_rows, u2):
    n = lf_rows.shape[0]
    blk = math.gcd(n, SUFFIX_ROWS)
    return pl.pallas_call(
        _suffix_kernel, grid=(n // blk,),
        in_specs=[pl.BlockSpec((blk, PAGE), lambda i: (i, 0)), _full((PAGE, 2 * PAGE))],
        out_specs=pl.BlockSpec((blk, 2 * PAGE), lambda i: (i, 0)),
        out_shape=jax.ShapeDtypeStruct((n, 2 * PAGE), F32),
        compiler_params=_cparams(("arbitrary",)), name="logf_suffix",
    )(lf_rows, u2)


def _decode_kernel(n_steps, pt_ref, qbd_ref, kn_ref, vn_ref, lfn_ref, uincl_ref, *rest):
    del pt_ref
    g_pages = PAGES_PER_STEP
    kc_refs = rest[0:g_pages]
    vc_refs = rest[g_pages:2 * g_pages]
    ds_refs = rest[2 * g_pages:3 * g_pages]
    o_ref, m_ref, l_ref, acc_ref, carry_ref, cn_ref = rest[3 * g_pages:]
    p = pl.program_id(1)
    rows = qbd_ref.shape[1]
    nh = ds_refs[0].shape[1]
    s_new = rows // nh

    def expand(t):
        return jnp.concatenate(
            [jnp.broadcast_to(t[hh:hh + 1, :], (s_new, t.shape[1])) for hh in range(nh)], axis=0)

    @pl.when(p == 0)
    def _():
        m_ref[...] = jnp.full_like(m_ref, NEG_BIG)
        l_ref[...] = jnp.zeros_like(l_ref)
        acc_ref[...] = jnp.zeros_like(acc_ref)
        carry_ref[...] = jnp.zeros_like(carry_ref)
        cn_ref[...] = jnp.dot(lfn_ref[0], uincl_ref[...], preferred_element_type=F32,
                              precision=HIGHEST)

    q = qbd_ref[0]
    cnx = expand(cn_ref[...])
    lane = lax.broadcasted_iota(jnp.int32, (rows, PAGE), 1)
    tq = lax.broadcasted_iota(jnp.int32, (rows, PAGE), 0) % s_new
    cn_col = jnp.sum(jnp.where(lane == tq, cnx, 0.0), axis=1, keepdims=True)

    def attend(kts, vts, biases, keep):
        ss = [jnp.dot(q, kt, preferred_element_type=F32) + b for kt, b in zip(kts, biases)]
        if keep is not None:
            ss = [jnp.where(keep, s, -jnp.inf) for s in ss]
        smax = functools.reduce(jnp.maximum, ss)
        m_prev = m_ref[...]
        m_new = jnp.maximum(m_prev, jnp.max(smax, axis=1, keepdims=True))
        alpha = jnp.exp(m_prev - m_new)
        prs = [jnp.exp(s - m_new) for s in ss]
        psum = functools.reduce(jnp.add, prs)
        l_ref[...] = alpha * l_ref[...] + jnp.sum(psum, axis=1, keepdims=True)
        pvs = [lax.dot_general(pr.astype(BF16), vt, (((1,), (1,)), ((), ())),
                               preferred_element_type=F32) for pr, vt in zip(prs, vts)]
        acc_ref[...] = alpha * acc_ref[...] + functools.reduce(jnp.add, pvs)
        m_ref[...] = m_new

    carry = carry_ref[...]
    biases = []
    for g in range(g_pages):
        ds = ds_refs[g][0]
        biases.append(expand(ds[:, :PAGE] + carry) + cn_col)
        carry = carry + ds[:, PAGE:PAGE + 1]
    carry_ref[...] = carry
    attend([r[...].astype(BF16) for r in kc_refs], [r[...].astype(BF16) for r in vc_refs],
           biases, None)

    @pl.when(p == n_steps - 1)
    def _():
        attend([kn_ref[0]], [vn_ref[0]], [cn_col - cnx], lane <= tq)
        out = acc_ref[...] / l_ref[...]
        width = out.shape[1]
        r = lax.broadcasted_iota(jnp.int32, (rows, width), 0) // s_new
        cidx = lax.broadcasted_iota(jnp.int32, (rows, width), 1) // HEAD_DIM
        out = jnp.where(r == cidx, out, 0.0)
        o_ref[0] = jnp.sum(out.reshape(nh, s_new, width), axis=0).astype(o_ref.dtype)


def _decode(pt_flat, qbd, kn_pad, vn_pad, lfn_t, kc, vc, dsuf, uincl, n_pages):
    bd_, rows, db = qbd.shape
    nh = dsuf.shape[1]
    s_new = rows // nh
    g_pages = PAGES_PER_STEP
    n_steps = n_pages // g_pages
    seqb = lambda b, p, pt: (b, 0, 0)

    def page(g):
        return lambda b, p, pt: (pt[b * n_pages + (n_pages - 1 - (p * g_pages + g))], 0, 0)

    in_specs = [
        pl.BlockSpec((1, rows, db), seqb),
        pl.BlockSpec((1, db, PAGE), seqb),
        pl.BlockSpec((1, db, PAGE), seqb),
        pl.BlockSpec((1, nh, PAGE), seqb),
        pl.BlockSpec((PAGE, PAGE), lambda b, p, pt: (0, 0)),
    ]
    in_specs += [pl.BlockSpec((None, db, PAGE), page(g)) for g in range(g_pages)]
    in_specs += [pl.BlockSpec((None, db, PAGE), page(g)) for g in range(g_pages)]
    in_specs += [pl.BlockSpec((1, nh, 2 * PAGE), page(g)) for g in range(g_pages)]
    grid_spec = pltpu.PrefetchScalarGridSpec(
        num_scalar_prefetch=1, grid=(bd_, n_steps), in_specs=in_specs,
        out_specs=pl.BlockSpec((1, s_new, db), seqb),
        scratch_shapes=[pltpu.VMEM((rows, 1), F32), pltpu.VMEM((rows, 1), F32),
                        pltpu.VMEM((rows, db), F32), pltpu.VMEM((nh, 1), F32),
                        pltpu.VMEM((nh, PAGE), F32)],
    )
    return pl.pallas_call(
        functools.partial(_decode_kernel, n_steps), grid_spec=grid_spec,
        out_shape=jax.ShapeDtypeStruct((bd_, s_new, db), BF16),
        compiler_params=_cparams(("arbitrary", "arbitrary")), name="fox_decode",
    )(pt_flat, qbd, kn_pad, vn_pad, lfn_t, uincl, *([kc] * g_pages), *([vc] * g_pages),
      *([dsuf] * g_pages))


def _outproj_kernel(n_prompt_tiles, x_ref, oa_ref, obp_ref, obs_ref, woa_ref, wob_ref, g2_ref,
                    wr_ref, br_ref, x1_ref, h2_ref, ti_ref, tg_ref):
    i = pl.program_id(0)
    ob = jnp.where(i >= n_prompt_tiles, obs_ref[...], obp_ref[...])
    mix = (jnp.dot(oa_ref[...], woa_ref[...], preferred_element_type=F32)
           + jnp.dot(ob, wob_ref[...], preferred_element_type=F32))
    x1 = x_ref[...] + mix
    x1_ref[...] = x1
    ms = jnp.mean(x1 * x1, axis=-1, keepdims=True)
    h2 = x1 * lax.rsqrt(ms + EPS) * g2_ref[...]
    h2_ref[...] = h2
    logits = jnp.dot(h2, wr_ref[...], preferred_element_type=F32, precision=HIGHEST) + br_ref[...]
    lane = lax.broadcasted_iota(jnp.int32, logits.shape, 1)
    cur = jnp.where(lane < N_EXPERTS, logits, -jnp.inf)
    idx_out = jnp.zeros(logits.shape, jnp.int32)
    val_out = jnp.zeros(logits.shape, F32)
    v0 = None
    denom = None
    for k in range(TOP_K):
        mval = jnp.max(cur, axis=1, keepdims=True)
        midx = jnp.min(jnp.where(cur == mval, lane, LANES), axis=1, keepdims=True)
        if k == 0:
            v0 = mval
            ek = jnp.ones_like(mval)
            denom = ek
        else:
            ek = jnp.exp(mval - v0)
            denom = denom + ek
        idx_out = jnp.where(lane == k, midx, idx_out)
        val_out = jnp.where(lane == k, ek, val_out)
        cur = jnp.where(lane == midx, -jnp.inf, cur)
    ti_ref[...] = idx_out
    tg_ref[...] = val_out / denom


def _outproj(x, oa, obp, obs, woa, wob, g2, wr, br, n_prompt_tiles):
    nt, d = x.shape
    half = oa.shape[1]
    row = lambda w: pl.BlockSpec((TM, w), lambda i: (i, 0))
    in_specs = [
        row(d), row(half),
        pl.BlockSpec((TM, half), lambda i: (jnp.minimum(i, n_prompt_tiles - 1), 0)),
        _full((TM, half)),
        _full((half, d)), _full((half, d)), _full((1, d)), _full((d, LANES)), _full((1, LANES)),
    ]
    out_shape = [jax.ShapeDtypeStruct((nt, d), F32), jax.ShapeDtypeStruct((nt, d), F32),
                 jax.ShapeDtypeStruct((nt, LANES), jnp.int32), jax.ShapeDtypeStruct((nt, LANES), F32)]
    return pl.pallas_call(
        functools.partial(_outproj_kernel, n_prompt_tiles),
        grid=(nt // TM,), in_specs=in_specs,
        out_specs=[row(d), row(d), row(LANES), row(LANES)], out_shape=out_shape,
        compiler_params=_cparams(("arbitrary",)), name="outproj_router",
    )(x, oa, obp, obs, woa, wob, g2, wr, br)


def _pack_pairs(x):
    n = x.shape[1] // 2
    lo = pltpu.bitcast(x[:, :n], jnp.uint32) >> 16
    hi = pltpu.bitcast(x[:, n:], jnp.uint32) & jnp.uint32(0xFFFF0000)
    return lo | hi


def _unpack_pairs(w):
    a = pltpu.bitcast(w << 16, F32)
    b = pltpu.bitcast(w & jnp.uint32(0xFFFF0000), F32)
    return jnp.concatenate([a, b], axis=1).astype(BF16)


def _dispatch_kernel(h2_ref, ti_ref, tg_ref, lstrict_ref, ustrict_ref, ones_ref,
                     xl_ref, pg_ref, cnt_ref):
    rows = h2_ref.shape[0]
    rp = xl_ref.shape[0]
    half = h2_ref.shape[1] // 2
    lane = lax.broadcasted_iota(jnp.int32, (rows, LANES), 1)
    ti = ti_ref[...]
    tg = tg_ref[...]
    chose = [lane == ti[:, k:k + 1] for k in range(TOP_K)]
    member = functools.reduce(jnp.logical_or, chose)
    e_mat = jnp.where(member, 1.0, 0.0)
    rank = jnp.dot(lstrict_ref[...], e_mat.astype(BF16), preferred_element_type=F32)
    cnt = rank[rows - 1:, :] + e_mat[rows - 1:, :]
    gran = jnp.floor((cnt + (GRANULE - 1)) * (1.0 / GRANULE))
    off = jnp.dot(jnp.broadcast_to(gran * GRANULE, (8, LANES)).astype(BF16), ustrict_ref[...],
                  preferred_element_type=F32)[:1, :]
    pos = off + rank
    pk = pltpu.roll(tg, TOP_K, axis=1)
    for k in range(TOP_K):
        pos_k = jnp.sum(jnp.where(chose[k], pos, 0.0), axis=1, keepdims=True)
        pk = jnp.where(lane == k, pos_k, pk)
    pg_ref[...] = pk
    cnt_ref[...] = jnp.broadcast_to(gran, (8, LANES))

    pkt = pk.T
    r = lax.broadcasted_iota(jnp.int32, (rp, rows), 0).astype(F32)
    sel = jnp.zeros((rp, rows), F32)
    gsel = jnp.zeros((rp, rows), F32)
    for k in range(TOP_K):
        hit = r == pkt[k:k + 1, :]
        sel = jnp.where(hit, 1.0, sel)
        gsel = jnp.where(hit, pkt[TOP_K + k:TOP_K + k + 1, :], gsel)
    xl = jnp.dot(sel.astype(BF16), h2_ref[...].astype(BF16), preferred_element_type=F32)
    g_hi = gsel.astype(BF16)
    g_lo = (gsel - g_hi.astype(F32)).astype(BF16)
    gate = (jnp.dot(g_hi, ones_ref[...], preferred_element_type=F32)
            + jnp.dot(g_lo, ones_ref[...], preferred_element_type=F32))
    xl_ref[:, :half] = _pack_pairs(xl)
    xl_ref[:, half:] = pltpu.bitcast(gate, jnp.uint32)


def _dispatch(h2, ti, tg, lstrict, ustrict, ones):
    nt, d = h2.shape
    n_tiles = nt // TM
    rp = TOP_K * TM + N_EXPERTS * GRANULE
    row = lambda w: pl.BlockSpec((TM, w), lambda i: (i, 0))
    return pl.pallas_call(
        _dispatch_kernel, grid=(n_tiles,),
        in_specs=[row(d), row(LANES), row(LANES), _full((TM, TM)), _full((LANES, LANES)),
                  _full((TM, LANES))],
        out_specs=[pl.BlockSpec((rp, d // 2 + LANES), lambda i: (i, 0)), row(LANES),
                   pl.BlockSpec((8, LANES), lambda i: (i, 0))],
        out_shape=[jax.ShapeDtypeStruct((n_tiles * rp, d // 2 + LANES), jnp.uint32),
                   jax.ShapeDtypeStruct((nt, LANES), F32),
                   jax.ShapeDtypeStruct((n_tiles * 8, LANES), F32)],
        compiler_params=_cparams(("arbitrary",)), name="dispatch",
    )(h2, ti, tg, lstrict, ustrict, ones)


def _granule_copies(idx_ref, base, hbm, buf, sem, to_hbm):
    copies = []
    for q in range(MOE_BLOCK // GRANULE):
        g = idx_ref[base + q]
        hrows = hbm.at[pl.ds(pl.multiple_of(g * GRANULE, GRANULE), GRANULE)]
        brows = buf.at[pl.ds(q * GRANULE, GRANULE)]
        copies.append(pltpu.make_async_copy(brows, hrows, sem) if to_hbm
                      else pltpu.make_async_copy(hrows, brows, sem))
    return copies


def _expert_kernel(be_ref, bv_ref, gsrc_ref, gdst_ref, xl_hbm, wgu_ref, bgu_ref, wdn_ref, bdn_ref,
                   yz_hbm, yl_hbm, xs_buf, y_buf, wgu_bf, wdn_bf, sem_in, sem_out):
    del yz_hbm
    i = pl.program_id(0)
    n_blocks = pl.num_programs(0)
    gpb = MOE_BLOCK // GRANULE
    d_ff = wdn_ref.shape[1]
    half = y_buf.shape[1]
    slot = i % 2

    def wait_in(s):
        pltpu.make_async_copy(xl_hbm.at[pl.ds(0, MOE_BLOCK)], xs_buf.at[s], sem_in.at[s]).wait()

    def wait_out():
        pltpu.make_async_copy(y_buf, yl_hbm.at[pl.ds(0, MOE_BLOCK)], sem_out.at[0]).wait()

    @pl.when((i == 0) & (bv_ref[0] == 1))
    def _():
        for c in _granule_copies(gsrc_ref, 0, xl_hbm, xs_buf.at[0], sem_in.at[0], False):
            c.start()

    @pl.when(bv_ref[i] == 1)
    def _():
        nxt = jnp.minimum(i + 1, n_blocks - 1)
        has_next = (i + 1 < n_blocks) & (bv_ref[nxt] == 1)

        @pl.when(has_next)
        def _():
            for c in _granule_copies(gsrc_ref, nxt * gpb, xl_hbm, xs_buf.at[1 - slot],
                                     sem_in.at[1 - slot], False):
                c.start()

        prev = be_ref[jnp.maximum(i - 1, 0)]

        @pl.when((i == 0) | (prev != be_ref[i]))
        def _():
            wgu_bf[...] = wgu_ref[0].astype(BF16)
            wdn_bf[...] = wdn_ref[0].astype(BF16)

        wait_in(slot)
        xs = _unpack_pairs(xs_buf[slot, :, :half])
        gate = pltpu.bitcast(xs_buf[slot, :, half:], F32)
        gu = jnp.dot(xs, wgu_bf[...], preferred_element_type=F32) + bgu_ref[0]
        glu = jnp.minimum(gu[:, :d_ff], SWIGLU_LIMIT)
        lin = jnp.clip(gu[:, d_ff:], -SWIGLU_LIMIT, SWIGLU_LIMIT)
        act = glu * jax.nn.sigmoid(SWIGLU_ALPHA * glu) * (lin + 1.0)
        y = jnp.dot(act.astype(BF16), wdn_bf[...], preferred_element_type=F32) + bdn_ref[0]
        y = y * jnp.concatenate([gate] * (y.shape[1] // LANES), axis=1)
        packed = _pack_pairs(y.astype(BF16).astype(F32))

        @pl.when(i > 0)
        def _():
            wait_out()

        y_buf[...] = packed
        for c in _granule_copies(gdst_ref, i * gpb, yl_hbm, y_buf, sem_out.at[0], True):
            c.start()

        @pl.when(jnp.logical_not(has_next))
        def _():
            wait_out()


def _experts(block_e, block_valid, gsrc, gdst, xl, yl_zero, wgu, bgu, wdn, bdn):
    n_blocks = block_e.shape[0]
    _, d, two_ff = wgu.shape
    d_ff = two_ff // 2
    eidx = lambda i, be, bv, gs, gd: (be[i], 0, 0)
    grid_spec = pltpu.PrefetchScalarGridSpec(
        num_scalar_prefetch=4, grid=(n_blocks,),
        in_specs=[
            pl.BlockSpec(memory_space=pl.ANY),
            pl.BlockSpec((1, d, two_ff), eidx),
            pl.BlockSpec((1, 1, two_ff), eidx),
            pl.BlockSpec((1, d_ff, d), eidx),
            pl.BlockSpec((1, 1, d), eidx),
            pl.BlockSpec(memory_space=pl.ANY),
        ],
        out_specs=pl.BlockSpec(memory_space=pl.ANY),
        scratch_shapes=[pltpu.VMEM((2, MOE_BLOCK, xl.shape[1]), jnp.uint32),
                        pltpu.VMEM((MOE_BLOCK, d // 2), jnp.uint32),
                        pltpu.VMEM((d, two_ff), BF16), pltpu.VMEM((d_ff, d), BF16),
                        pltpu.SemaphoreType.DMA((2,)), pltpu.SemaphoreType.DMA((1,))],
    )
    return pl.pallas_call(
        _expert_kernel, grid_spec=grid_spec,
        out_shape=jax.ShapeDtypeStruct(yl_zero.shape, jnp.uint32),
        input_output_aliases={9: 0},
        compiler_params=_cparams(("arbitrary",)), name="experts",
    )(block_e, block_valid, gsrc, gdst, xl, wgu, bgu, wdn, bdn, yl_zero)


def _combine_kernel(x1_ref, pg_ref, yl_ref, o_ref):
    rows = x1_ref.shape[0]
    rp = yl_ref.shape[0]
    pg = pg_ref[...]
    r = lax.broadcasted_iota(jnp.int32, (rows, rp), 1).astype(F32)
    sel = jnp.zeros((rows, rp), F32)
    for k in range(TOP_K):
        sel = jnp.where(r == pg[:, k:k + 1], 1.0, sel)
    yl = _unpack_pairs(yl_ref[...])
    o_ref[...] = x1_ref[...] + jnp.dot(sel.astype(BF16), yl, preferred_element_type=F32)


def _combine(x1, pg, yl):
    nt, d = x1.shape
    n_tiles = nt // TM
    rp = TOP_K * TM + N_EXPERTS * GRANULE
    row = lambda w: pl.BlockSpec((TM, w), lambda i: (i, 0))
    return pl.pallas_call(
        _combine_kernel, grid=(n_tiles,),
        in_specs=[row(d), row(LANES), pl.BlockSpec((rp, d // 2), lambda i: (i, 0))],
        out_specs=row(d),
        out_shape=jax.ShapeDtypeStruct((nt, d), F32),
        compiler_params=_cparams(("arbitrary",)), name="combine",
    )(x1, pg, yl)


def _route(cnt_rows, n_tiles, nt):
    i32 = jnp.int32
    cg = cnt_rows.reshape(n_tiles, 8, LANES)[:, 0, :N_EXPERTS].astype(i32)
    gp_tile = (TOP_K * TM + N_EXPERTS * GRANULE) // GRANULE
    gpb = MOE_BLOCK // GRANULE
    seg = jnp.arange(n_tiles, dtype=i32)[:, None] * gp_tile + jnp.cumsum(cg, axis=1) - cg
    cum_incl = jnp.cumsum(cg, axis=0)
    cum_excl = cum_incl - cg
    ng = cum_incl[-1]
    nb = (ng + gpb - 1) // gpb
    bend = jnp.cumsum(nb)
    bstart = bend - nb
    max_gran = (nt * TOP_K) // GRANULE + n_tiles * N_EXPERTS
    n_blocks = -(-max_gran // gpb) + N_EXPERTS
    bi = jnp.arange(n_blocks, dtype=i32)
    block_e = jnp.minimum(jnp.sum((bend[None, :] <= bi[:, None]).astype(i32), axis=1),
                          N_EXPERTS - 1).astype(i32)
    block_valid = (bi < bend[-1]).astype(i32)
    onehot = (block_e[:, None] == jnp.arange(N_EXPERTS, dtype=i32)[None, :]).astype(F32)
    pick = lambda table: jnp.dot(onehot, table.astype(F32), precision=HIGHEST)
    cum_rows = pick(cum_incl.T)
    base_rows = pick((seg - cum_excl).T)
    u0 = (bi.astype(F32) - pick(bstart)) * gpb
    q = jnp.arange(gpb, dtype=i32)
    u = u0[:, None] + q[None, :].astype(F32)
    valid = (u < pick(ng)[:, None]) & (block_valid[:, None] == 1)
    tile_of = jnp.minimum(jnp.sum((cum_rows[:, None, :] <= u[:, :, None]).astype(i32), axis=2),
                          n_tiles - 1)
    in_tile = tile_of[:, :, None] == jnp.arange(n_tiles, dtype=i32)[None, None, :]
    g = (jnp.sum(jnp.where(in_tile, base_rows[:, None, :], 0.0), axis=2) + u).astype(i32)
    gsrc = jnp.where(valid, g, 0).reshape(-1)
    gdst = jnp.where(valid, g, n_tiles * gp_tile + q[None, :]).reshape(-1)
    return block_e, block_valid, gsrc.astype(i32), gdst.astype(i32)


def _bias_selector(nh, db):
    rows = jnp.arange(LANES)[:, None]
    cols = jnp.arange(2 * db)[None, :]
    head = cols // LANES
    off = cols % LANES - BIAS_LANE
    part = rows // nh
    is_split = (part < N_SPLIT) & (rows % nh == head)
    is_one = rows == N_SPLIT * nh
    q_sel = ((is_split & (off == part)) | (is_one & (off >= N_SPLIT) & (off < 2 * N_SPLIT)))
    k_sel = (is_one & (off >= 0) & (off < N_SPLIT)).astype(F32) \
        - (is_split & (off == part + N_SPLIT)).astype(F32)
    return jnp.concatenate([q_sel.astype(F32), k_sel], axis=1).astype(BF16)


def kernel(x_prompt, x_sample, cache_k, cache_v, cache_logf, page_table, norm1_g, w_in, b_f, gv_g,
           w_s, b_s, q_g, k_g, w_out, norm2_g, w_router, b_router, w_gu, b_gu, w_dn, b_dn):
    batch, seq, d = x_prompt.shape
    dec_b, dec_s, _ = x_sample.shape
    depth = w_in.shape[0]
    ha = gv_g.shape[1]
    hb = b_f.shape[1]
    da, db = ha * HEAD_DIM, hb * HEAD_DIM
    n_main = 2 * da + 3 * db
    n_phys = cache_k.shape[1]
    n_pages = page_table.shape[1]
    t_prompt = batch * seq
    t_dec = dec_b * dec_s
    assert t_dec == TM and seq % TM == 0 and TM % CHUNK == 0 and seq % BQ == 0 and BQ == BK
    assert da % LANES == 0 and db % LANES == 0 and dec_s <= CHUNK and TM % dec_s == 0
    assert cache_k.shape[2] == PAGE and n_pages % PAGES_PER_STEP == 0
    assert (N_SPLIT + 1) * hb <= LANES
    n_prompt_tiles = t_prompt // TM
    tiles_per_seq = seq // TM

    x = jnp.concatenate([x_prompt.reshape(t_prompt, d), x_sample.reshape(t_dec, d)], axis=0)

    wm_all = w_in[:, :, :n_main].astype(BF16)
    wf_all = jnp.pad(w_in[:, :, n_main:], ((0, 0), (0, 0), (0, LANES - hb))).astype(BF16)
    bf_all = jnp.pad(b_f, ((0, 0), (0, LANES - hb))).reshape(depth, 1, LANES)
    gv_all = gv_g.reshape(depth, 1, da)
    qg_all = jnp.tile(q_g, (1, hb)).reshape(depth, 1, db)
    kg_all = jnp.tile(k_g, (1, hb)).reshape(depth, 1, db)
    head_of = jnp.arange(db) // HEAD_DIM
    bd = (head_of[:, None] == head_of[None, :]).astype(BF16)
    tril = jnp.tril(jnp.ones((CHUNK, CHUNK), F32))
    wt = w_s * tril
    eye_p = jnp.eye(TM // CHUNK, dtype=F32)
    eye_s = jnp.eye(TM // dec_s, dtype=F32)
    wblk_p = jnp.einsum('ab,lhts->lhatbs', eye_p, wt).reshape(depth, ha, TM, TM)
    wblk_s = jnp.einsum('ab,lhts->lhatbs', eye_s, wt[:, :, :dec_s, :dec_s]).reshape(depth, ha, TM, TM)
    wblk_all = jnp.stack([wblk_p, wblk_s], axis=1).astype(BF16)
    bs_t = jnp.repeat(jnp.swapaxes(b_s, 1, 2), HEAD_DIM, axis=2)
    bias_p = jnp.tile(bs_t, (1, TM // CHUNK, 1))
    bias_s = jnp.tile(bs_t[:, :dec_s], (1, TM // dec_s, 1))
    bias_all = jnp.stack([bias_p, bias_s], axis=1)
    ltri = jnp.tril(jnp.ones((TM, TM), F32))
    sel = _bias_selector(hb, db)
    woa_all = w_out[:, :da].astype(BF16)
    wob_all = w_out[:, da:].astype(BF16)
    wr_all = jnp.pad(w_router, ((0, 0), (0, 0), (0, LANES - N_EXPERTS)))
    br_all = jnp.pad(b_router, ((0, 0), (0, LANES - N_EXPERTS))).reshape(depth, 1, LANES)
    two_ff = w_gu.shape[3]
    wgu_all = w_gu.reshape(depth * N_EXPERTS, d, two_ff)
    bgu_all = b_gu.reshape(depth * N_EXPERTS, 1, two_ff)
    wdn_all = w_dn.reshape(depth * N_EXPERTS, two_ff // 2, d)
    bdn_all = b_dn.reshape(depth * N_EXPERTS, 1, d)
    nt = t_prompt + t_dec
    n_tiles = nt // TM
    ar_t = jnp.arange(TM)
    lstrict = (ar_t[:, None] > ar_t[None, :]).astype(BF16)
    ar_l = jnp.arange(LANES)
    ustrict_e = (ar_l[:, None] < ar_l[None, :]).astype(BF16)
    ones_tm = jnp.ones((TM, LANES), BF16)
    sorted_rows = n_tiles * (TOP_K * TM + N_EXPERTS * GRANULE)
    yl_zero = jnp.zeros((sorted_rows + MOE_BLOCK, d // 2), jnp.uint32)

    kc = jnp.transpose(cache_k, (0, 1, 3, 4, 2)).reshape(depth * n_phys, db, PAGE)
    vc = jnp.transpose(cache_v, (0, 1, 3, 4, 2)).reshape(depth * n_phys, db, PAGE)
    lfc_rows = jnp.swapaxes(cache_logf, 2, 3).reshape(depth * n_phys * hb, PAGE)
    pt = page_table.reshape(dec_b * n_pages).astype(jnp.int32)
    ar = jnp.arange(PAGE)
    ustrict = (ar[:, None] > ar[None, :]).astype(F32)
    uincl = (ar[:, None] <= ar[None, :]).astype(F32)
    u2 = jnp.concatenate([ustrict, jnp.ones((PAGE, PAGE), F32)], axis=1)
    dsuf = _suffix(lfc_rows, u2).reshape(depth * n_phys, hb, 2 * PAGE)
    eye_h = jnp.eye(hb, dtype=F32)

    kstack = jnp.zeros((depth, batch, db, seq), F32)
    vstack = jnp.zeros((depth, batch, db, seq), F32)
    lstack = jnp.zeros((depth, batch, hb, seq), F32)

    ks_l, vs_l, fs_l, cv_l = [], [], [], []
    for l in range(depth):
        oa, qa, kt, va, kstack, vstack, lstack, va_s, q_s, k_s, v_s, lf_s = _project(
            l, x, norm1_g[l].reshape(1, d), wm_all[l], wf_all[l], bf_all[l], gv_all[l], qg_all[l],
            kg_all[l], bd, wblk_all[l], bias_all[l], ltri, sel, kstack, vstack, lstack,
            n_prompt_tiles, tiles_per_seq)

        ob_p = _flash(qa, kt, va, batch, seq)

        q4 = q_s.reshape(dec_b, dec_s, hb, HEAD_DIM) * (HEAD_DIM ** -0.5)
        qbd = jnp.einsum('bthd,hg->bhtgd', q4, eye_h).reshape(dec_b, hb * dec_s, db).astype(BF16)
        k_new = k_s.reshape(dec_b, dec_s, db)
        v_new = v_s.reshape(dec_b, dec_s, db)
        lf_new = lf_s.reshape(dec_b, dec_s, hb)
        padr = ((0, 0), (0, 0), (0, PAGE - dec_s))
        kn_pad = jnp.pad(jnp.swapaxes(k_new, 1, 2), padr).astype(BF16)
        vn_pad = jnp.pad(jnp.swapaxes(v_new, 1, 2), padr).astype(BF16)
        lfn_t = jnp.pad(jnp.swapaxes(lf_new, 1, 2), ((0, 0), (0, 0), (0, PAGE - dec_s)))
        ob_s = _decode(pt + l * n_phys, qbd, kn_pad, vn_pad, lfn_t, kc, vc, dsuf, uincl,
                       n_pages).reshape(t_dec, db)

        x1, h2, ti, tg = _outproj(x, oa, ob_p, ob_s, woa_all[l], wob_all[l],
                                  norm2_g[l].reshape(1, d), wr_all[l], br_all[l], n_prompt_tiles)
        xl, pg, cnt_rows = _dispatch(h2, ti, tg, lstrict, ustrict_e, ones_tm)
        block_e, block_valid, gsrc, gdst = _route(cnt_rows, n_tiles, nt)
        yl = _experts(block_e + l * N_EXPERTS, block_valid, gsrc, gdst, xl, yl_zero, wgu_all,
                      bgu_all, wdn_all, bdn_all)
        x = _combine(x1, pg, yl)

        ks_l.append(k_new.reshape(dec_b, dec_s, hb, HEAD_DIM))
        vs_l.append(v_new.reshape(dec_b, dec_s, hb, HEAD_DIM))
        fs_l.append(lf_new)
        cv_l.append(va_s.reshape(dec_b, dec_s, ha, HEAD_DIM))

    y_prompt = x[:t_prompt].reshape(batch, seq, d)
    y_sample = x[t_prompt:].reshape(dec_b, dec_s, d)
    k_prompt = jnp.transpose(kstack.reshape(depth, batch, hb, HEAD_DIM, seq), (0, 1, 4, 2, 3))
    v_prompt = jnp.transpose(vstack.reshape(depth, batch, hb, HEAD_DIM, seq), (0, 1, 4, 2, 3))
    logf_prompt = jnp.transpose(lstack, (0, 1, 3, 2))
    return (y_prompt, y_sample, k_prompt, v_prompt, logf_prompt,
            jnp.stack(ks_l), jnp.stack(vs_l), jnp.stack(fs_l), jnp.stack(cv_l))
```

```python
import functools
import math

import jax
import jax.numpy as jnp
from jax import lax
from jax.experimental import pallas as pl
from jax.experimental.pallas import tpu as pltpu

F32 = jnp.float32
BF16 = jnp.bfloat16
HIGHEST = lax.Precision.HIGHEST

HEAD_DIM = 64
CHUNK = 128
PAGE = 128
N_EXPERTS = 32
TOP_K = 4
EPS = 1e-6
SWIGLU_ALPHA = 1.702
SWIGLU_LIMIT = 7.0
LOG2E = math.log2(math.e)
NEG_BIG = -1e30

LANES = 128
TM = 256
BQ = 512
BK = 512
QC = 128
KC = 256
PAGES_PER_STEP = 32
SUFFIX_ROWS = 4096
MOE_BLOCK = 256
GRANULE = 8
VMEM_LIMIT = 56 * 1024 * 1024

BIAS_LANE = HEAD_DIM
N_SPLIT = 3


def _cparams(sem):
    return pltpu.CompilerParams(dimension_semantics=sem, vmem_limit_bytes=VMEM_LIMIT)


def _full(shape):
    nd = len(shape)
    return pl.BlockSpec(shape, lambda *_: (0,) * nd)


def _proj_kernel(tiles_per_seq, n_prompt_tiles,
                 x_ref, g1_ref, wm_ref, wf_ref, bf_ref, gv_ref, qg_ref, kg_ref, bd_ref,
                 wblk_ref, bias_ref, ltri_ref, sel_ref, kin_ref, vin_ref, lin_ref,
                 oa_ref, qa_ref, kt_ref, va_ref, kst_ref, vst_ref, lst_ref,
                 vn_ref, qs_ref, ks_ref, vs_ref, ls_ref, carry_ref):
    del kin_ref, vin_ref, lin_ref
    i = pl.program_id(0)
    da = gv_ref.shape[1]
    db = qg_ref.shape[1]
    x = x_ref[...]
    ms = jnp.mean(x * x, axis=-1, keepdims=True)
    h = (x * lax.rsqrt(ms + EPS) * g1_ref[...]).astype(BF16)
    p = jnp.dot(h, wm_ref[...], preferred_element_type=F32)
    fl = jnp.dot(h, wf_ref[...], preferred_element_type=F32)

    def group_norm(t, gamma):
        sq = (t * t).astype(BF16)
        gms = jnp.dot(sq, bd_ref[...], preferred_element_type=F32) * (1.0 / HEAD_DIM)
        return t * lax.rsqrt(gms + EPS) * gamma

    u = jax.nn.gelu(p[:, 0:da])
    van = group_norm(jax.nn.gelu(p[:, da:2 * da]), gv_ref[...])
    qn = group_norm(p[:, 2 * da:2 * da + db], qg_ref[...])
    kn = group_norm(p[:, 2 * da + db:2 * da + 2 * db], kg_ref[...])
    vv = p[:, 2 * da + 2 * db:2 * da + 3 * db]

    vn_ref[...] = van
    qs_ref[...] = qn
    ks_ref[...] = kn
    vs_ref[...] = vv

    lane = lax.broadcasted_iota(jnp.int32, (x.shape[0], LANES), 1)
    low = lane < HEAD_DIM

    van_bf = van.astype(BF16)
    for j in range(da // LANES):
        rhs = van_bf[:, j * LANES:(j + 1) * LANES]
        z0 = jnp.dot(wblk_ref[0, 2 * j], rhs, preferred_element_type=F32)
        z1 = jnp.dot(wblk_ref[0, 2 * j + 1], rhs, preferred_element_type=F32)
        z = jnp.where(low, z0, z1) + bias_ref[0, :, j * LANES:(j + 1) * LANES]
        oa_ref[:, j * LANES:(j + 1) * LANES] = (u[:, j * LANES:(j + 1) * LANES] * z).astype(BF16)

    nh = ls_ref.shape[1]
    z = fl + bf_ref[...]
    lf = jnp.minimum(z, 0.0) - jnp.log1p(jnp.exp(-jnp.abs(z)))
    lf = jnp.where(lane < nh, lf, 0.0)
    ls_ref[...] = lf[:, :nh]

    @pl.when(i % tiles_per_seq == 0)
    def _():
        carry_ref[...] = jnp.zeros_like(carry_ref)

    c = jnp.dot(ltri_ref[...], lf, preferred_element_type=F32, precision=HIGHEST) + carry_ref[...]
    carry_ref[...] = c[c.shape[0] - 1:, :]

    c2 = c * LOG2E
    hi = c2.astype(BF16).astype(F32)
    r1 = c2 - hi
    mid = r1.astype(BF16).astype(F32)
    lo = (r1 - mid).astype(BF16).astype(F32)
    c3 = (hi + pltpu.roll(mid, nh, axis=1) + pltpu.roll(lo, 2 * nh, axis=1)
          + jnp.where(lane == N_SPLIT * nh, 1.0, 0.0)).astype(BF16)
    extra = jnp.dot(c3, sel_ref[...], preferred_element_type=F32)

    qscale = (HEAD_DIM ** -0.5) * LOG2E
    kparts = []
    for j in range(db // LANES):
        qp = qn[:, j * LANES:(j + 1) * LANES] * qscale
        kp = kn[:, j * LANES:(j + 1) * LANES]
        vp = vv[:, j * LANES:(j + 1) * LANES]
        for half, (qh, kh, vh) in enumerate(((qp, kp, vp),
                                             (pltpu.roll(qp, HEAD_DIM, axis=1),
                                              pltpu.roll(kp, HEAD_DIM, axis=1),
                                              pltpu.roll(vp, HEAD_DIM, axis=1)))):
            hh = 2 * j + half
            cols = slice(hh * LANES, (hh + 1) * LANES)
            kcols = slice(2 * db + hh * LANES, 2 * db + (hh + 1) * LANES)
            qa_ref[:, cols] = jnp.where(low, qh, extra[:, cols]).astype(BF16)
            va_ref[:, cols] = jnp.where(low, vh, 1.0).astype(BF16)
            kparts.append(jnp.where(low, kh, extra[:, kcols]))
    ktf = jnp.concatenate(kparts, axis=1).T
    kt_ref[...] = ktf.astype(BF16)

    @pl.when(i < n_prompt_tiles)
    def _():
        kst_ref[...] = jnp.concatenate(
            [ktf[hh * LANES:hh * LANES + HEAD_DIM, :] for hh in range(db // HEAD_DIM)], axis=0)
        vst_ref[...] = vv.T
        lst_ref[...] = lf.T[:nh, :]


def _project(layer, x, g1, wm, wf, bfp, gv, qg, kg, bd, wblk, bias, ltri, sel, kstack, vstack, lstack,
             n_prompt_tiles, tiles_per_seq):
    nt, d = x.shape
    n_main = wm.shape[1]
    da, db = gv.shape[1], qg.shape[1]
    nh = db // HEAD_DIM
    grid = (nt // TM,)
    row = lambda w: pl.BlockSpec((TM, w), lambda i: (i, 0))
    variant = lambda i: i // n_prompt_tiles
    any_spec = pl.BlockSpec(memory_space=pl.ANY)

    def seq_block(i):
        ip = jnp.minimum(i, n_prompt_tiles - 1)
        return (layer, ip // tiles_per_seq, 0, ip % tiles_per_seq)

    in_specs = [
        row(d), _full((1, d)), _full((d, n_main)), _full((d, LANES)), _full((1, LANES)),
        _full((1, da)), _full((1, db)), _full((1, db)), _full((db, db)),
        pl.BlockSpec((1, da // HEAD_DIM, TM, TM), lambda i: (variant(i), 0, 0, 0)),
        pl.BlockSpec((1, TM, da), lambda i: (variant(i), 0, 0)),
        _full((TM, TM)), _full((LANES, 4 * db)), any_spec, any_spec, any_spec,
    ]
    out_shape = [
        jax.ShapeDtypeStruct((nt, da), BF16),
        jax.ShapeDtypeStruct((nt, 2 * db), BF16),
        jax.ShapeDtypeStruct((2 * db, nt), BF16),
        jax.ShapeDtypeStruct((nt, 2 * db), BF16),
        jax.ShapeDtypeStruct(kstack.shape, F32),
        jax.ShapeDtypeStruct(vstack.shape, F32),
        jax.ShapeDtypeStruct(lstack.shape, F32),
        jax.ShapeDtypeStruct((TM, da), F32),
        jax.ShapeDtypeStruct((TM, db), F32),
        jax.ShapeDtypeStruct((TM, db), F32),
        jax.ShapeDtypeStruct((TM, db), F32),
        jax.ShapeDtypeStruct((TM, nh), F32),
    ]
    out_specs = [row(da), row(2 * db), pl.BlockSpec((2 * db, TM), lambda i: (0, i)), row(2 * db),
                 pl.BlockSpec((None, None, db, TM), seq_block),
                 pl.BlockSpec((None, None, db, TM), seq_block),
                 pl.BlockSpec((None, None, nh, TM), seq_block),
                 _full((TM, da)), _full((TM, db)), _full((TM, db)), _full((TM, db)), _full((TM, nh))]
    return pl.pallas_call(
        functools.partial(_proj_kernel, tiles_per_seq, n_prompt_tiles),
        grid=grid, in_specs=in_specs, out_specs=out_specs, out_shape=out_shape,
        scratch_shapes=[pltpu.VMEM((1, LANES), F32)],
        input_output_aliases={13: 4, 14: 5, 15: 6},
        compiler_params=_cparams(("arbitrary",)), name="proj",
    )(x, g1, wm, wf, bfp, gv, qg, kg, bd, wblk, bias, ltri, sel, kstack, vstack, lstack)


def _flash_kernel(qa_ref, kt_ref, va_ref, o_ref, m_ref, acc_ref):
    qi = pl.program_id(1)
    ki = pl.program_id(2)
    nh = va_ref.shape[1] // LANES

    @pl.when(ki == 0)
    def _():
        m_ref[...] = jnp.full_like(m_ref, NEG_BIG)
        acc_ref[...] = jnp.zeros_like(acc_ref)

    def step(diag):
        for hh in range(nh):
            for qc in range(BQ // QC):
                rows = slice(qc * QC, (qc + 1) * QC)
                q = qa_ref[rows, hh * LANES:(hh + 1) * LANES]
                m = m_ref[hh, rows, :]
                acc = acc_ref[hh, rows, :]
                for kc in range(BK // KC):
                    if diag and kc * KC > qc * QC + QC - 1:
                        continue
                    cols = slice(kc * KC, (kc + 1) * KC)
                    s = jnp.dot(q, kt_ref[hh * LANES:(hh + 1) * LANES, cols],
                                preferred_element_type=F32)
                    if diag and kc * KC + KC - 1 > qc * QC:
                        r = lax.broadcasted_iota(jnp.int32, (QC, KC), 0) + qc * QC
                        c = lax.broadcasted_iota(jnp.int32, (QC, KC), 1) + kc * KC
                        s = jnp.where(c <= r, s, -jnp.inf)
                    smax = jnp.maximum(s[:, :LANES], s[:, LANES:])
                    m_new = jnp.maximum(m, jnp.max(smax, axis=1, keepdims=True))
                    alpha = jnp.exp2(m - m_new)
                    pr = jnp.concatenate([jnp.exp2(s[:, :LANES] - m_new),
                                          jnp.exp2(s[:, LANES:] - m_new)], axis=1)
                    acc = alpha * acc + jnp.dot(pr.astype(BF16),
                                                va_ref[cols, hh * LANES:(hh + 1) * LANES],
                                                preferred_element_type=F32)
                    m = m_new
                m_ref[hh, rows, :] = m
                acc_ref[hh, rows, :] = acc

    @pl.when(ki < qi)
    def _():
        step(False)

    @pl.when(ki == qi)
    def _():
        step(True)
        lane = lax.broadcasted_iota(jnp.int32, (BQ, LANES), 1)
        for j in range(nh // 2):
            a0 = acc_ref[2 * j]
            a1 = acc_ref[2 * j + 1]
            num = jnp.where(lane < HEAD_DIM, a0, pltpu.roll(a1, HEAD_DIM, axis=1))
            den = jnp.where(lane < HEAD_DIM, pltpu.roll(a0, HEAD_DIM, axis=1), a1)
            o_ref[:, j * LANES:(j + 1) * LANES] = (num / den).astype(o_ref.dtype)


def _flash(qa, kt, va, batch, seq):
    wide = qa.shape[1]
    db = wide // 2
    nh = db // HEAD_DIM
    nq = seq // BQ
    nk = seq // BK
    qrow = lambda b, qi, ki: (b * nq + qi, 0)
    in_specs = [
        pl.BlockSpec((BQ, wide), qrow),
        pl.BlockSpec((wide, BK), lambda b, qi, ki: (0, b * nk + jnp.minimum(ki, qi))),
        pl.BlockSpec((BK, wide), lambda b, qi, ki: (b * nk + jnp.minimum(ki, qi), 0)),
    ]
    return pl.pallas_call(
        _flash_kernel,
        grid=(batch, nq, nk), in_specs=in_specs,
        out_specs=pl.BlockSpec((BQ, db), qrow),
        out_shape=jax.ShapeDtypeStruct((batch * seq, db), BF16),
        scratch_shapes=[pltpu.VMEM((nh, BQ, LANES), F32), pltpu.VMEM((nh, BQ, LANES), F32)],
        compiler_params=_cparams(("arbitrary", "arbitrary", "arbitrary")), name="fox_prompt",
    )(qa, kt, va)


def _suffix_kernel(lf_ref, u2_ref, o_ref):
    o_ref[...] = jnp.dot(lf_ref[...], u2_ref[...], preferred_element_type=F32, precision=HIGHEST)


def _suffix(lf_rows, u2):
    n = lf_rows.shape[0]
    blk = math.gcd(n, SUFFIX_ROWS)
    return pl.pallas_call(
        _suffix_kernel, grid=(n // blk,),
        in_specs=[pl.BlockSpec((blk, PAGE), lambda i: (i, 0)), _full((PAGE, 2 * PAGE))],
        out_specs=pl.BlockSpec((blk, 2 * PAGE), lambda i: (i, 0)),
        out_shape=jax.ShapeDtypeStruct((n, 2 * PAGE), F32),
        compiler_params=_cparams(("arbitrary",)), name="logf_suffix",
    )(lf_rows, u2)


def _decode_kernel(n_steps, pt_ref, qbd_ref, kn_ref, vn_ref, lfn_ref, uincl_ref, *rest):
    del pt_ref
    g_pages = PAGES_PER_STEP
    kc_refs = rest[0:g_pages]
    vc_refs = rest[g_pages:2 * g_pages]
    ds_refs = rest[2 * g_pages:3 * g_pages]
    o_ref, m_ref, l_ref, acc_ref, carry_ref, cn_ref = rest[3 * g_pages:]
    p = pl.program_id(1)
    rows = qbd_ref.shape[1]
    nh = ds_refs[0].shape[1]
    s_new = rows // nh

    def expand(t):
        return jnp.concatenate(
            [jnp.broadcast_to(t[hh:hh + 1, :], (s_new, t.shape[1])) for hh in range(nh)], axis=0)

    @pl.when(p == 0)
    def _():
        m_ref[...] = jnp.full_like(m_ref, NEG_BIG)
        l_ref[...] = jnp.zeros_like(l_ref)
        acc_ref[...] = jnp.zeros_like(acc_ref)
        carry_ref[...] = jnp.zeros_like(carry_ref)
        cn_ref[...] = jnp.dot(lfn_ref[0], uincl_ref[...], preferred_element_type=F32,
                              precision=HIGHEST)

    q = qbd_ref[0]
    cnx = expand(cn_ref[...])
    lane = lax.broadcasted_iota(jnp.int32, (rows, PAGE), 1)
    tq = lax.broadcasted_iota(jnp.int32, (rows, PAGE), 0) % s_new
    cn_col = jnp.sum(jnp.where(lane == tq, cnx, 0.0), axis=1, keepdims=True)

    def attend(kts, vts, biases, keep):
        ss = [jnp.dot(q, kt, preferred_element_type=F32) + b for kt, b in zip(kts, biases)]
        if keep is not None:
            ss = [jnp.where(keep, s, -jnp.inf) for s in ss]
        smax = functools.reduce(jnp.maximum, ss)
        m_prev = m_ref[...]
        m_new = jnp.maximum(m_prev, jnp.max(smax, axis=1, keepdims=True))
        alpha = jnp.exp(m_prev - m_new)
        prs = [jnp.exp(s - m_new) for s in ss]
        psum = functools.reduce(jnp.add, prs)
        l_ref[...] = alpha * l_ref[...] + jnp.sum(psum, axis=1, keepdims=True)
        pvs = [lax.dot_general(pr.astype(BF16), vt, (((1,), (1,)), ((), ())),
                               preferred_element_type=F32) for pr, vt in zip(prs, vts)]
        acc_ref[...] = alpha * acc_ref[...] + functools.reduce(jnp.add, pvs)
        m_ref[...] = m_new

    carry = carry_ref[...]
    biases = []
    for g in range(g_pages):
        ds = ds_refs[g][0]
        biases.append(expand(ds[:, :PAGE] + carry) + cn_col)
        carry = carry + ds[:, PAGE:PAGE + 1]
    carry_ref[...] = carry
    attend([r[...].astype(BF16) for r in kc_refs], [r[...].astype(BF16) for r in vc_refs],
           biases, None)

    @pl.when(p == n_steps - 1)
    def _():
        attend([kn_ref[0]], [vn_ref[0]], [cn_col - cnx], lane <= tq)
        out = acc_ref[...] / l_ref[...]
        width = out.shape[1]
        r = lax.broadcasted_iota(jnp.int32, (rows, width), 0) // s_new
        cidx = lax.broadcasted_iota(jnp.int32, (rows, width), 1) // HEAD_DIM
        out = jnp.where(r == cidx, out, 0.0)
        o_ref[0] = jnp.sum(out.reshape(nh, s_new, width), axis=0).astype(o_ref.dtype)


def _decode(pt_flat, qbd, kn_pad, vn_pad, lfn_t, kc, vc, dsuf, uincl, n_pages):
    bd_, rows, db = qbd.shape
    nh = dsuf.shape[1]
    s_new = rows // nh
    g_pages = PAGES_PER_STEP
    n_steps = n_pages // g_pages
    seqb = lambda b, p, pt: (b, 0, 0)

    def page(g):
        return lambda b, p, pt: (pt[b * n_pages + (n_pages - 1 - (p * g_pages + g))], 0, 0)

    in_specs = [
        pl.BlockSpec((1, rows, db), seqb),
        pl.BlockSpec((1, db, PAGE), seqb),
        pl.BlockSpec((1, db, PAGE), seqb),
        pl.BlockSpec((1, nh, PAGE), seqb),
        pl.BlockSpec((PAGE, PAGE), lambda b, p, pt: (0, 0)),
    ]
    in_specs += [pl.BlockSpec((None, db, PAGE), page(g)) for g in range(g_pages)]
    in_specs += [pl.BlockSpec((None, db, PAGE), page(g)) for g in range(g_pages)]
    in_specs += [pl.BlockSpec((1, nh, 2 * PAGE), page(g)) for g in range(g_pages)]
    grid_spec = pltpu.PrefetchScalarGridSpec(
        num_scalar_prefetch=1, grid=(bd_, n_steps), in_specs=in_specs,
        out_specs=pl.BlockSpec((1, s_new, db), seqb),
        scratch_shapes=[pltpu.VMEM((rows, 1), F32), pltpu.VMEM((rows, 1), F32),
                        pltpu.VMEM((rows, db), F32), pltpu.VMEM((nh, 1), F32),
                        pltpu.VMEM((nh, PAGE), F32)],
    )
    return pl.pallas_call(
        functools.partial(_decode_kernel, n_steps), grid_spec=grid_spec,
        out_shape=jax.ShapeDtypeStruct((bd_, s_new, db), BF16),
        compiler_params=_cparams(("arbitrary", "arbitrary")), name="fox_decode",
    )(pt_flat, qbd, kn_pad, vn_pad, lfn_t, uincl, *([kc] * g_pages), *([vc] * g_pages),
      *([dsuf] * g_pages))


def _outproj_kernel(n_prompt_tiles, x_ref, oa_ref, obp_ref, obs_ref, woa_ref, wob_ref, g2_ref,
                    wr_ref, br_ref, x1_ref, h2_ref, ti_ref, tg_ref):
    i = pl.program_id(0)
    ob = jnp.where(i >= n_prompt_tiles, obs_ref[...], obp_ref[...])
    mix = (jnp.dot(oa_ref[...], woa_ref[...], preferred_element_type=F32)
           + jnp.dot(ob, wob_ref[...], preferred_element_type=F32))
    x1 = x_ref[...] + mix
    x1_ref[...] = x1
    ms = jnp.mean(x1 * x1, axis=-1, keepdims=True)
    h2 = x1 * lax.rsqrt(ms + EPS) * g2_ref[...]
    h2_ref[...] = h2
    h_hi = h2.astype(BF16)
    h_lo = (h2 - h_hi.astype(F32)).astype(BF16)
    wr = wr_ref[...]
    w_hi = wr.astype(BF16)
    w_lo = (wr - w_hi.astype(F32)).astype(BF16)
    logits = (jnp.dot(h_hi, w_hi, preferred_element_type=F32)
              + jnp.dot(h_hi, w_lo, preferred_element_type=F32)
              + jnp.dot(h_lo, w_hi, preferred_element_type=F32)) + br_ref[...]
    lane = lax.broadcasted_iota(jnp.int32, logits.shape, 1)
    cur = jnp.where(lane < N_EXPERTS, logits, -jnp.inf)
    idx_out = jnp.zeros(logits.shape, jnp.int32)
    val_out = jnp.zeros(logits.shape, F32)
    v0 = None
    denom = None
    for k in range(TOP_K):
        mval = jnp.max(cur, axis=1, keepdims=True)
        midx = jnp.min(jnp.where(cur == mval, lane, LANES), axis=1, keepdims=True)
        if k == 0:
            v0 = mval
            ek = jnp.ones_like(mval)
            denom = ek
        else:
            ek = jnp.exp(mval - v0)
            denom = denom + ek
        idx_out = jnp.where(lane == k, midx, idx_out)
        val_out = jnp.where(lane == k, ek, val_out)
        cur = jnp.where(lane == midx, -jnp.inf, cur)
    ti_ref[...] = idx_out
    tg_ref[...] = val_out / denom


def _outproj(x, oa, obp, obs, woa, wob, g2, wr, br, n_prompt_tiles):
    nt, d = x.shape
    half = oa.shape[1]
    row = lambda w: pl.BlockSpec((TM, w), lambda i: (i, 0))
    in_specs = [
        row(d), row(half),
        pl.BlockSpec((TM, half), lambda i: (jnp.minimum(i, n_prompt_tiles - 1), 0)),
        _full((TM, half)),
        _full((half, d)), _full((half, d)), _full((1, d)), _full((d, LANES)), _full((1, LANES)),
    ]
    out_shape = [jax.ShapeDtypeStruct((nt, d), F32), jax.ShapeDtypeStruct((nt, d), F32),
                 jax.ShapeDtypeStruct((nt, LANES), jnp.int32), jax.ShapeDtypeStruct((nt, LANES), F32)]
    return pl.pallas_call(
        functools.partial(_outproj_kernel, n_prompt_tiles),
        grid=(nt // TM,), in_specs=in_specs,
        out_specs=[row(d), row(d), row(LANES), row(LANES)], out_shape=out_shape,
        compiler_params=_cparams(("arbitrary",)), name="outproj_router",
    )(x, oa, obp, obs, woa, wob, g2, wr, br)


def _pack_pairs(x):
    n = x.shape[1] // 2
    lo = pltpu.bitcast(x[:, :n], jnp.uint32) >> 16
    hi = pltpu.bitcast(x[:, n:], jnp.uint32) & jnp.uint32(0xFFFF0000)
    return lo | hi


def _unpack_pairs(w):
    a = pltpu.bitcast(w << 16, F32)
    b = pltpu.bitcast(w & jnp.uint32(0xFFFF0000), F32)
    return jnp.concatenate([a, b], axis=1).astype(BF16)


def _dispatch_kernel(h2_ref, ti_ref, tg_ref, lstrict_ref, ustrict_ref, ones_ref,
                     xl_ref, pg_ref, cnt_ref, yl_ref):
    yl_ref[...] = jnp.zeros_like(yl_ref)
    rows = h2_ref.shape[0]
    rp = xl_ref.shape[0]
    half = h2_ref.shape[1] // 2
    lane = lax.broadcasted_iota(jnp.int32, (rows, LANES), 1)
    ti = ti_ref[...]
    tg = tg_ref[...]
    chose = [lane == ti[:, k:k + 1] for k in range(TOP_K)]
    member = functools.reduce(jnp.logical_or, chose)
    e_mat = jnp.where(member, 1.0, 0.0)
    rank = jnp.dot(lstrict_ref[...], e_mat.astype(BF16), preferred_element_type=F32)
    cnt = rank[rows - 1:, :] + e_mat[rows - 1:, :]
    gran = jnp.floor((cnt + (GRANULE - 1)) * (1.0 / GRANULE))
    off = jnp.dot(jnp.broadcast_to(gran * GRANULE, (8, LANES)).astype(BF16), ustrict_ref[...],
                  preferred_element_type=F32)[:1, :]
    pos = off + rank
    pk = pltpu.roll(tg, TOP_K, axis=1)
    for k in range(TOP_K):
        pos_k = jnp.sum(jnp.where(chose[k], pos, 0.0), axis=1, keepdims=True)
        pk = jnp.where(lane == k, pos_k, pk)
    pg_ref[...] = pk
    cnt_ref[...] = jnp.broadcast_to(gran, (8, LANES))

    pkt = pk.T
    r = lax.broadcasted_iota(jnp.int32, (rp, rows), 0).astype(F32)
    sel = jnp.zeros((rp, rows), F32)
    gsel = jnp.zeros((rp, rows), F32)
    for k in range(TOP_K):
        hit = r == pkt[k:k + 1, :]
        sel = jnp.where(hit, 1.0, sel)
        gsel = jnp.where(hit, pkt[TOP_K + k:TOP_K + k + 1, :], gsel)
    xl = jnp.dot(sel.astype(BF16), h2_ref[...].astype(BF16), preferred_element_type=F32)
    g_hi = gsel.astype(BF16)
    g_lo = (gsel - g_hi.astype(F32)).astype(BF16)
    gate = (jnp.dot(g_hi, ones_ref[...], preferred_element_type=F32)
            + jnp.dot(g_lo, ones_ref[...], preferred_element_type=F32))
    xl_ref[:, :half] = _pack_pairs(xl)
    xl_ref[:, half:] = pltpu.bitcast(gate, jnp.uint32)


def _dispatch(h2, ti, tg, lstrict, ustrict, ones):
    nt, d = h2.shape
    n_tiles = nt // TM
    rp = TOP_K * TM + N_EXPERTS * GRANULE
    row = lambda w: pl.BlockSpec((TM, w), lambda i: (i, 0))
    return pl.pallas_call(
        _dispatch_kernel, grid=(n_tiles,),
        in_specs=[row(d), row(LANES), row(LANES), _full((TM, TM)), _full((LANES, LANES)),
                  _full((TM, LANES))],
        out_specs=[pl.BlockSpec((rp, d // 2 + LANES), lambda i: (i, 0)), row(LANES),
                   pl.BlockSpec((8, LANES), lambda i: (i, 0)),
                   pl.BlockSpec((rp, d // 2), lambda i: (i, 0))],
        out_shape=[jax.ShapeDtypeStruct((n_tiles * rp, d // 2 + LANES), jnp.uint32),
                   jax.ShapeDtypeStruct((nt, LANES), F32),
                   jax.ShapeDtypeStruct((n_tiles * 8, LANES), F32),
                   jax.ShapeDtypeStruct((n_tiles * rp, d // 2), jnp.uint32)],
        compiler_params=_cparams(("arbitrary",)), name="dispatch",
    )(h2, ti, tg, lstrict, ustrict, ones)


def _granule_copies(idx_ref, base, hbm, buf, sem, to_hbm):
    copies = []
    for q in range(MOE_BLOCK // GRANULE):
        g = idx_ref[base + q]
        hrows = hbm.at[pl.ds(pl.multiple_of(g * GRANULE, GRANULE), GRANULE)]
        brows = buf.at[pl.ds(q * GRANULE, GRANULE)]
        copies.append(pltpu.make_async_copy(brows, hrows, sem) if to_hbm
                      else pltpu.make_async_copy(hrows, brows, sem))
    return copies


def _expert_kernel(be_ref, nv_ref, gsrc_ref, gdst_ref, xl_hbm, wgu_ref, bgu_ref, wdn_ref, bdn_ref,
                   yz_hbm, yl_hbm, xs_buf, y_buf, wgu_bf, wdn_bf, sem_in, sem_out):
    del yz_hbm
    i = pl.program_id(0)
    n_blocks = pl.num_programs(0)
    gpb = MOE_BLOCK // GRANULE
    d_ff = wdn_ref.shape[1]
    half = y_buf.shape[1]
    slot = i % 2

    def wait_in(s):
        pltpu.make_async_copy(xl_hbm.at[pl.ds(0, MOE_BLOCK)], xs_buf.at[s], sem_in.at[s]).wait()

    def for_out_copies(blk, fn):
        copies = _granule_copies(gdst_ref, blk * gpb, yl_hbm, y_buf, sem_out.at[0], True)
        n = nv_ref[blk]

        @pl.when(n == gpb)
        def _():
            for c in copies:
                fn(c)

        @pl.when(n < gpb)
        def _():
            for q, c in enumerate(copies):
                @pl.when(q < n)
                def _(c=c):
                    fn(c)

    @pl.when((i == 0) & (nv_ref[0] > 0))
    def _():
        for c in _granule_copies(gsrc_ref, 0, xl_hbm, xs_buf.at[0], sem_in.at[0], False):
            c.start()

    @pl.when(nv_ref[i] > 0)
    def _():
        nxt = jnp.minimum(i + 1, n_blocks - 1)
        has_next = (i + 1 < n_blocks) & (nv_ref[nxt] > 0)

        @pl.when(has_next)
        def _():
            for c in _granule_copies(gsrc_ref, nxt * gpb, xl_hbm, xs_buf.at[1 - slot],
                                     sem_in.at[1 - slot], False):
                c.start()

        prev = be_ref[jnp.maximum(i - 1, 0)]

        @pl.when((i == 0) | (prev != be_ref[i]))
        def _():
            wgu_bf[...] = wgu_ref[0].astype(BF16)
            wdn_bf[...] = wdn_ref[0].astype(BF16)

        wait_in(slot)
        xs = _unpack_pairs(xs_buf[slot, :, :half])
        gate = pltpu.bitcast(xs_buf[slot, :, half:], F32)
        gu = jnp.dot(xs, wgu_bf[...], preferred_element_type=F32) + bgu_ref[0]
        glu = jnp.minimum(gu[:, :d_ff], SWIGLU_LIMIT)
        lin = jnp.clip(gu[:, d_ff:], -SWIGLU_LIMIT, SWIGLU_LIMIT)
        act = glu * jax.nn.sigmoid(SWIGLU_ALPHA * glu) * (lin + 1.0)
        y = jnp.dot(act.astype(BF16), wdn_bf[...], preferred_element_type=F32) + bdn_ref[0]
        y = y * jnp.concatenate([gate] * (y.shape[1] // LANES), axis=1)
        packed = _pack_pairs(y.astype(BF16).astype(F32))

        @pl.when(i > 0)
        def _():
            for_out_copies(jnp.maximum(i - 1, 0), lambda c: c.wait())

        y_buf[...] = packed
        for_out_copies(i, lambda c: c.start())

        @pl.when(jnp.logical_not(has_next))
        def _():
            for_out_copies(i, lambda c: c.wait())


def _experts(block_e, block_valid, gsrc, gdst, xl, yl_zero, wgu, bgu, wdn, bdn):
    n_blocks = block_e.shape[0]
    _, d, two_ff = wgu.shape
    d_ff = two_ff // 2
    eidx = lambda i, be, bv, gs, gd: (be[i], 0, 0)
    grid_spec = pltpu.PrefetchScalarGridSpec(
        num_scalar_prefetch=4, grid=(n_blocks,),
        in_specs=[
            pl.BlockSpec(memory_space=pl.ANY),
            pl.BlockSpec((1, d, two_ff), eidx),
            pl.BlockSpec((1, 1, two_ff), eidx),
            pl.BlockSpec((1, d_ff, d), eidx),
            pl.BlockSpec((1, 1, d), eidx),
            pl.BlockSpec(memory_space=pl.ANY),
        ],
        out_specs=pl.BlockSpec(memory_space=pl.ANY),
        scratch_shapes=[pltpu.VMEM((2, MOE_BLOCK, xl.shape[1]), jnp.uint32),
                        pltpu.VMEM((MOE_BLOCK, d // 2), jnp.uint32),
                        pltpu.VMEM((d, two_ff), BF16), pltpu.VMEM((d_ff, d), BF16),
                        pltpu.SemaphoreType.DMA((2,)), pltpu.SemaphoreType.DMA((1,))],
    )
    return pl.pallas_call(
        _expert_kernel, grid_spec=grid_spec,
        out_shape=jax.ShapeDtypeStruct(yl_zero.shape, jnp.uint32),
        input_output_aliases={9: 0},
        compiler_params=_cparams(("arbitrary",)), name="experts",
    )(block_e, block_valid, gsrc, gdst, xl, wgu, bgu, wdn, bdn, yl_zero)


def _combine_kernel(x1_ref, pg_ref, yl_ref, o_ref):
    rows = x1_ref.shape[0]
    rp = yl_ref.shape[0]
    pg = pg_ref[...]
    r = lax.broadcasted_iota(jnp.int32, (rows, rp), 1).astype(F32)
    sel = jnp.zeros((rows, rp), F32)
    for k in range(TOP_K):
        sel = jnp.where(r == pg[:, k:k + 1], 1.0, sel)
    yl = _unpack_pairs(yl_ref[...])
    o_ref[...] = x1_ref[...] + jnp.dot(sel.astype(BF16), yl, preferred_element_type=F32)


def _combine(x1, pg, yl):
    nt, d = x1.shape
    n_tiles = nt // TM
    rp = TOP_K * TM + N_EXPERTS * GRANULE
    row = lambda w: pl.BlockSpec((TM, w), lambda i: (i, 0))
    return pl.pallas_call(
        _combine_kernel, grid=(n_tiles,),
        in_specs=[row(d), row(LANES), pl.BlockSpec((rp, d // 2), lambda i: (i, 0))],
        out_specs=row(d),
        out_shape=jax.ShapeDtypeStruct((nt, d), F32),
        compiler_params=_cparams(("arbitrary",)), name="combine",
    )(x1, pg, yl)


def _route(cnt_rows, n_tiles, nt):
    i32 = jnp.int32
    cg = cnt_rows.reshape(n_tiles, 8, LANES)[:, 0, :N_EXPERTS].astype(i32)
    gp_tile = (TOP_K * TM + N_EXPERTS * GRANULE) // GRANULE
    gpb = MOE_BLOCK // GRANULE
    seg = jnp.arange(n_tiles, dtype=i32)[:, None] * gp_tile + jnp.cumsum(cg, axis=1) - cg
    cum_incl = jnp.cumsum(cg, axis=0)
    cum_excl = cum_incl - cg
    ng = cum_incl[-1]
    nb = (ng + gpb - 1) // gpb
    bend = jnp.cumsum(nb)
    bstart = bend - nb
    max_gran = (nt * TOP_K) // GRANULE + n_tiles * N_EXPERTS
    n_blocks = -(-max_gran // gpb) + N_EXPERTS
    bi = jnp.arange(n_blocks, dtype=i32)
    block_e = jnp.minimum(jnp.sum((bend[None, :] <= bi[:, None]).astype(i32), axis=1),
                          N_EXPERTS - 1).astype(i32)
    block_valid = (bi < bend[-1]).astype(i32)
    onehot = (block_e[:, None] == jnp.arange(N_EXPERTS, dtype=i32)[None, :]).astype(F32)
    pick = lambda table: jnp.dot(onehot, table.astype(F32), precision=HIGHEST)
    cum_rows = pick(cum_incl.T)
    base_rows = pick((seg - cum_excl).T)
    u0 = (bi.astype(F32) - pick(bstart)) * gpb
    q = jnp.arange(gpb, dtype=i32)
    u = u0[:, None] + q[None, :].astype(F32)
    valid = (u < pick(ng)[:, None]) & (block_valid[:, None] == 1)
    tile_of = jnp.minimum(jnp.sum((cum_rows[:, None, :] <= u[:, :, None]).astype(i32), axis=2),
                          n_tiles - 1)
    in_tile = tile_of[:, :, None] == jnp.arange(n_tiles, dtype=i32)[None, None, :]
    g = (jnp.sum(jnp.where(in_tile, base_rows[:, None, :], 0.0), axis=2) + u).astype(i32)
    gidx = jnp.where(valid, g, 0).reshape(-1).astype(i32)
    block_nv = jnp.sum(valid.astype(i32), axis=1)
    return block_e, block_nv, gidx


def _bias_selector(nh, db):
    rows = jnp.arange(LANES)[:, None]
    cols = jnp.arange(2 * db)[None, :]
    head = cols // LANES
    off = cols % LANES - BIAS_LANE
    part = rows // nh
    is_split = (part < N_SPLIT) & (rows % nh == head)
    is_one = rows == N_SPLIT * nh
    q_sel = ((is_split & (off == part)) | (is_one & (off >= N_SPLIT) & (off < 2 * N_SPLIT)))
    k_sel = (is_one & (off >= 0) & (off < N_SPLIT)).astype(F32) \
        - (is_split & (off == part + N_SPLIT)).astype(F32)
    return jnp.concatenate([q_sel.astype(F32), k_sel], axis=1).astype(BF16)


def kernel(x_prompt, x_sample, cache_k, cache_v, cache_logf, page_table, norm1_g, w_in, b_f, gv_g,
           w_s, b_s, q_g, k_g, w_out, norm2_g, w_router, b_router, w_gu, b_gu, w_dn, b_dn):
    batch, seq, d = x_prompt.shape
    dec_b, dec_s, _ = x_sample.shape
    depth = w_in.shape[0]
    ha = gv_g.shape[1]
    hb = b_f.shape[1]
    da, db = ha * HEAD_DIM, hb * HEAD_DIM
    n_main = 2 * da + 3 * db
    n_phys = cache_k.shape[1]
    n_pages = page_table.shape[1]
    t_prompt = batch * seq
    t_dec = dec_b * dec_s
    assert t_dec == TM and seq % TM == 0 and TM % CHUNK == 0 and seq % BQ == 0 and BQ == BK
    assert da % LANES == 0 and db % LANES == 0 and dec_s <= CHUNK and TM % dec_s == 0
    assert cache_k.shape[2] == PAGE and n_pages % PAGES_PER_STEP == 0
    assert (N_SPLIT + 1) * hb <= LANES
    n_prompt_tiles = t_prompt // TM
    tiles_per_seq = seq // TM

    x = jnp.concatenate([x_prompt.reshape(t_prompt, d), x_sample.reshape(t_dec, d)], axis=0)

    wm_all = w_in[:, :, :n_main].astype(BF16)
    wf_all = jnp.pad(w_in[:, :, n_main:], ((0, 0), (0, 0), (0, LANES - hb))).astype(BF16)
    bf_all = jnp.pad(b_f, ((0, 0), (0, LANES - hb))).reshape(depth, 1, LANES)
    gv_all = gv_g.reshape(depth, 1, da)
    qg_all = jnp.tile(q_g, (1, hb)).reshape(depth, 1, db)
    kg_all = jnp.tile(k_g, (1, hb)).reshape(depth, 1, db)
    head_of = jnp.arange(db) // HEAD_DIM
    bd = (head_of[:, None] == head_of[None, :]).astype(BF16)
    tril = jnp.tril(jnp.ones((CHUNK, CHUNK), F32))
    wt = w_s * tril
    eye_p = jnp.eye(TM // CHUNK, dtype=F32)
    eye_s = jnp.eye(TM // dec_s, dtype=F32)
    wblk_p = jnp.einsum('ab,lhts->lhatbs', eye_p, wt).reshape(depth, ha, TM, TM)
    wblk_s = jnp.einsum('ab,lhts->lhatbs', eye_s, wt[:, :, :dec_s, :dec_s]).reshape(depth, ha, TM, TM)
    wblk_all = jnp.stack([wblk_p, wblk_s], axis=1).astype(BF16)
    bs_t = jnp.repeat(jnp.swapaxes(b_s, 1, 2), HEAD_DIM, axis=2)
    bias_p = jnp.tile(bs_t, (1, TM // CHUNK, 1))
    bias_s = jnp.tile(bs_t[:, :dec_s], (1, TM // dec_s, 1))
    bias_all = jnp.stack([bias_p, bias_s], axis=1)
    ltri = jnp.tril(jnp.ones((TM, TM), F32))
    sel = _bias_selector(hb, db)
    woa_all = w_out[:, :da].astype(BF16)
    wob_all = w_out[:, da:].astype(BF16)
    wr_all = jnp.pad(w_router, ((0, 0), (0, 0), (0, LANES - N_EXPERTS)))
    br_all = jnp.pad(b_router, ((0, 0), (0, LANES - N_EXPERTS))).reshape(depth, 1, LANES)
    two_ff = w_gu.shape[3]
    wgu_all = w_gu.reshape(depth * N_EXPERTS, d, two_ff)
    bgu_all = b_gu.reshape(depth * N_EXPERTS, 1, two_ff)
    wdn_all = w_dn.reshape(depth * N_EXPERTS, two_ff // 2, d)
    bdn_all = b_dn.reshape(depth * N_EXPERTS, 1, d)
    nt = t_prompt + t_dec
    n_tiles = nt // TM
    ar_t = jnp.arange(TM)
    lstrict = (ar_t[:, None] > ar_t[None, :]).astype(BF16)
    ar_l = jnp.arange(LANES)
    ustrict_e = (ar_l[:, None] < ar_l[None, :]).astype(BF16)
    ones_tm = jnp.ones((TM, LANES), BF16)

    kc = jnp.transpose(cache_k, (0, 1, 3, 4, 2)).reshape(depth * n_phys, db, PAGE)
    vc = jnp.transpose(cache_v, (0, 1, 3, 4, 2)).reshape(depth * n_phys, db, PAGE)
    lfc_rows = jnp.swapaxes(cache_logf, 2, 3).reshape(depth * n_phys * hb, PAGE)
    pt = page_table.reshape(dec_b * n_pages).astype(jnp.int32)
    ar = jnp.arange(PAGE)
    ustrict = (ar[:, None] > ar[None, :]).astype(F32)
    uincl = (ar[:, None] <= ar[None, :]).astype(F32)
    u2 = jnp.concatenate([ustrict, jnp.ones((PAGE, PAGE), F32)], axis=1)
    dsuf = _suffix(lfc_rows, u2).reshape(depth * n_phys, hb, 2 * PAGE)
    eye_h = jnp.eye(hb, dtype=F32)

    kstack = jnp.zeros((depth, batch, db, seq), F32)
    vstack = jnp.zeros((depth, batch, db, seq), F32)
    lstack = jnp.zeros((depth, batch, hb, seq), F32)

    ks_l, vs_l, fs_l, cv_l = [], [], [], []
    for l in range(depth):
        oa, qa, kt, va, kstack, vstack, lstack, va_s, q_s, k_s, v_s, lf_s = _project(
            l, x, norm1_g[l].reshape(1, d), wm_all[l], wf_all[l], bf_all[l], gv_all[l], qg_all[l],
            kg_all[l], bd, wblk_all[l], bias_all[l], ltri, sel, kstack, vstack, lstack,
            n_prompt_tiles, tiles_per_seq)

        ob_p = _flash(qa, kt, va, batch, seq)

        q4 = q_s.reshape(dec_b, dec_s, hb, HEAD_DIM) * (HEAD_DIM ** -0.5)
        qbd = jnp.einsum('bthd,hg->bhtgd', q4, eye_h).reshape(dec_b, hb * dec_s, db).astype(BF16)
        k_new = k_s.reshape(dec_b, dec_s, db)
        v_new = v_s.reshape(dec_b, dec_s, db)
        lf_new = lf_s.reshape(dec_b, dec_s, hb)
        padr = ((0, 0), (0, 0), (0, PAGE - dec_s))
        kn_pad = jnp.pad(jnp.swapaxes(k_new, 1, 2), padr).astype(BF16)
        vn_pad = jnp.pad(jnp.swapaxes(v_new, 1, 2), padr).astype(BF16)
        lfn_t = jnp.pad(jnp.swapaxes(lf_new, 1, 2), ((0, 0), (0, 0), (0, PAGE - dec_s)))
        ob_s = _decode(pt + l * n_phys, qbd, kn_pad, vn_pad, lfn_t, kc, vc, dsuf, uincl,
                       n_pages).reshape(t_dec, db)

        x1, h2, ti, tg = _outproj(x, oa, ob_p, ob_s, woa_all[l], wob_all[l],
                                  norm2_g[l].reshape(1, d), wr_all[l], br_all[l], n_prompt_tiles)
        xl, pg, cnt_rows, yl_init = _dispatch(h2, ti, tg, lstrict, ustrict_e, ones_tm)
        block_e, block_nv, gidx = _route(cnt_rows, n_tiles, nt)
        yl = _experts(block_e + l * N_EXPERTS, block_nv, gidx, gidx, xl, yl_init, wgu_all,
                      bgu_all, wdn_all, bdn_all)
        x = _combine(x1, pg, yl)

        ks_l.append(k_new.reshape(dec_b, dec_s, hb, HEAD_DIM))
        vs_l.append(v_new.reshape(dec_b, dec_s, hb, HEAD_DIM))
        fs_l.append(lf_new)
        cv_l.append(va_s.reshape(dec_b, dec_s, ha, HEAD_DIM))

    y_prompt = x[:t_prompt].reshape(batch, seq, d)
    y_sample = x[t_prompt:].reshape(dec_b, dec_s, d)
    k_prompt = jnp.transpose(kstack.reshape(depth, batch, hb, HEAD_DIM, seq), (0, 1, 4, 2, 3))
    v_prompt = jnp.transpose(vstack.reshape(depth, batch, hb, HEAD_DIM, seq), (0, 1, 4, 2, 3))
    logf_prompt = jnp.transpose(lstack, (0, 1, 3, 2))
    return (y_prompt, y_sample, k_prompt, v_prompt, logf_prompt,
            jnp.stack(ks_l), jnp.stack(vs_l), jnp.stack(fs_l), jnp.stack(cv_l))
```

```python
import functools
import math

import jax
import jax.numpy as jnp
from jax import lax
from jax.experimental import pallas as pl
from jax.experimental.pallas import tpu as pltpu

F32 = jnp.float32
BF16 = jnp.bfloat16
HIGHEST = lax.Precision.HIGHEST

HEAD_DIM = 64
CHUNK = 128
PAGE = 128
N_EXPERTS = 32
TOP_K = 4
EPS = 1e-6
SWIGLU_ALPHA = 1.702
SWIGLU_LIMIT = 7.0
LOG2E = math.log2(math.e)
NEG_BIG = -1e30

LANES = 128
TM = 256
BQ = 512
BK = 512
QC = 256
KC = 128
PAGES_PER_STEP = 32
SUFFIX_ROWS = 4096
MOE_BLOCK = 256
GRANULE = 8
VMEM_LIMIT = 56 * 1024 * 1024

BIAS_LANE = HEAD_DIM
N_SPLIT = 3


def _cparams(sem):
    return pltpu.CompilerParams(dimension_semantics=sem, vmem_limit_bytes=VMEM_LIMIT)


def _full(shape):
    nd = len(shape)
    return pl.BlockSpec(shape, lambda *_: (0,) * nd)


def _proj_kernel(tiles_per_seq, n_prompt_tiles,
                 x_ref, g1_ref, wm_ref, wf_ref, bf_ref, gv_ref, qg_ref, kg_ref, bd_ref,
                 wblk_ref, bias_ref, ltri_ref, sel_ref, kin_ref, vin_ref, lin_ref,
                 oa_ref, qa_ref, kt_ref, va_ref, kst_ref, vst_ref, lst_ref,
                 vn_ref, qs_ref, ks_ref, vs_ref, ls_ref, carry_ref):
    del kin_ref, vin_ref, lin_ref
    i = pl.program_id(0)
    da = gv_ref.shape[1]
    db = qg_ref.shape[1]
    x = x_ref[...]
    ms = jnp.mean(x * x, axis=-1, keepdims=True)
    h = (x * lax.rsqrt(ms + EPS) * g1_ref[...]).astype(BF16)
    p = jnp.dot(h, wm_ref[...], preferred_element_type=F32)
    fl = jnp.dot(h, wf_ref[...], preferred_element_type=F32)

    def group_norm(t, gamma):
        sq = (t * t).astype(BF16)
        gms = jnp.dot(sq, bd_ref[...], preferred_element_type=F32) * (1.0 / HEAD_DIM)
        return t * lax.rsqrt(gms + EPS) * gamma

    u = jax.nn.gelu(p[:, 0:da])
    van = group_norm(jax.nn.gelu(p[:, da:2 * da]), gv_ref[...])
    qn = group_norm(p[:, 2 * da:2 * da + db], qg_ref[...])
    kn = group_norm(p[:, 2 * da + db:2 * da + 2 * db], kg_ref[...])
    vv = p[:, 2 * da + 2 * db:2 * da + 3 * db]

    vn_ref[...] = van
    qs_ref[...] = qn
    ks_ref[...] = kn
    vs_ref[...] = vv

    lane = lax.broadcasted_iota(jnp.int32, (x.shape[0], LANES), 1)
    low = lane < HEAD_DIM

    van_bf = van.astype(BF16)
    for j in range(da // LANES):
        rhs = van_bf[:, j * LANES:(j + 1) * LANES]
        z0 = jnp.dot(wblk_ref[0, 2 * j], rhs, preferred_element_type=F32)
        z1 = jnp.dot(wblk_ref[0, 2 * j + 1], rhs, preferred_element_type=F32)
        z = jnp.where(low, z0, z1) + bias_ref[0, :, j * LANES:(j + 1) * LANES]
        oa_ref[:, j * LANES:(j + 1) * LANES] = (u[:, j * LANES:(j + 1) * LANES] * z).astype(BF16)

    nh = ls_ref.shape[1]
    z = fl + bf_ref[...]
    lf = jnp.minimum(z, 0.0) - jnp.log1p(jnp.exp(-jnp.abs(z)))
    lf = jnp.where(lane < nh, lf, 0.0)
    ls_ref[...] = lf[:, :nh]

    @pl.when(i % tiles_per_seq == 0)
    def _():
        carry_ref[...] = jnp.zeros_like(carry_ref)

    c = jnp.dot(ltri_ref[...], lf, preferred_element_type=F32, precision=HIGHEST) + carry_ref[...]
    carry_ref[...] = c[c.shape[0] - 1:, :]

    c2 = c * LOG2E
    hi = c2.astype(BF16).astype(F32)
    r1 = c2 - hi
    mid = r1.astype(BF16).astype(F32)
    lo = (r1 - mid).astype(BF16).astype(F32)
    c3 = (hi + pltpu.roll(mid, nh, axis=1) + pltpu.roll(lo, 2 * nh, axis=1)
          + jnp.where(lane == N_SPLIT * nh, 1.0, 0.0)).astype(BF16)
    extra = jnp.dot(c3, sel_ref[...], preferred_element_type=F32)

    qscale = (HEAD_DIM ** -0.5) * LOG2E
    kparts = []
    for j in range(db // LANES):
        qp = qn[:, j * LANES:(j + 1) * LANES] * qscale
        kp = kn[:, j * LANES:(j + 1) * LANES]
        vp = vv[:, j * LANES:(j + 1) * LANES]
        for half, (qh, kh, vh) in enumerate(((qp, kp, vp),
                                             (pltpu.roll(qp, HEAD_DIM, axis=1),
                                              pltpu.roll(kp, HEAD_DIM, axis=1),
                                              pltpu.roll(vp, HEAD_DIM, axis=1)))):
            hh = 2 * j + half
            cols = slice(hh * LANES, (hh + 1) * LANES)
            kcols = slice(2 * db + hh * LANES, 2 * db + (hh + 1) * LANES)
            qa_ref[:, cols] = jnp.where(low, qh, extra[:, cols]).astype(BF16)
            va_ref[:, cols] = jnp.where(low, vh, 1.0).astype(BF16)
            kparts.append(jnp.where(low, kh, extra[:, kcols]))
    ktf = jnp.concatenate(kparts, axis=1).T
    kt_ref[...] = ktf.astype(BF16)

    @pl.when(i < n_prompt_tiles)
    def _():
        kst_ref[...] = jnp.concatenate(
            [ktf[hh * LANES:hh * LANES + HEAD_DIM, :] for hh in range(db // HEAD_DIM)], axis=0)
        vst_ref[...] = vv.T
        lst_ref[...] = lf.T[:nh, :]


def _project(layer, x, g1, wm, wf, bfp, gv, qg, kg, bd, wblk, bias, ltri, sel, kstack, vstack, lstack,
             n_prompt_tiles, tiles_per_seq):
    nt, d = x.shape
    n_main = wm.shape[1]
    da, db = gv.shape[1], qg.shape[1]
    nh = db // HEAD_DIM
    grid = (nt // TM,)
    row = lambda w: pl.BlockSpec((TM, w), lambda i: (i, 0))
    variant = lambda i: i // n_prompt_tiles
    any_spec = pl.BlockSpec(memory_space=pl.ANY)

    def seq_block(i):
        ip = jnp.minimum(i, n_prompt_tiles - 1)
        return (layer, ip // tiles_per_seq, 0, ip % tiles_per_seq)

    in_specs = [
        row(d), _full((1, d)), _full((d, n_main)), _full((d, LANES)), _full((1, LANES)),
        _full((1, da)), _full((1, db)), _full((1, db)), _full((db, db)),
        pl.BlockSpec((1, da // HEAD_DIM, TM, TM), lambda i: (variant(i), 0, 0, 0)),
        pl.BlockSpec((1, TM, da), lambda i: (variant(i), 0, 0)),
        _full((TM, TM)), _full((LANES, 4 * db)), any_spec, any_spec, any_spec,
    ]
    out_shape = [
        jax.ShapeDtypeStruct((nt, da), BF16),
        jax.ShapeDtypeStruct((nt, 2 * db), BF16),
        jax.ShapeDtypeStruct((2 * db, nt), BF16),
        jax.ShapeDtypeStruct((nt, 2 * db), BF16),
        jax.ShapeDtypeStruct(kstack.shape, F32),
        jax.ShapeDtypeStruct(vstack.shape, F32),
        jax.ShapeDtypeStruct(lstack.shape, F32),
        jax.ShapeDtypeStruct((TM, da), F32),
        jax.ShapeDtypeStruct((TM, db), F32),
        jax.ShapeDtypeStruct((TM, db), F32),
        jax.ShapeDtypeStruct((TM, db), F32),
        jax.ShapeDtypeStruct((TM, nh), F32),
    ]
    out_specs = [row(da), row(2 * db), pl.BlockSpec((2 * db, TM), lambda i: (0, i)), row(2 * db),
                 pl.BlockSpec((None, None, db, TM), seq_block),
                 pl.BlockSpec((None, None, db, TM), seq_block),
                 pl.BlockSpec((None, None, nh, TM), seq_block),
                 _full((TM, da)), _full((TM, db)), _full((TM, db)), _full((TM, db)), _full((TM, nh))]
    return pl.pallas_call(
        functools.partial(_proj_kernel, tiles_per_seq, n_prompt_tiles),
        grid=grid, in_specs=in_specs, out_specs=out_specs, out_shape=out_shape,
        scratch_shapes=[pltpu.VMEM((1, LANES), F32)],
        input_output_aliases={13: 4, 14: 5, 15: 6},
        compiler_params=_cparams(("arbitrary",)), name="proj",
    )(x, g1, wm, wf, bfp, gv, qg, kg, bd, wblk, bias, ltri, sel, kstack, vstack, lstack)


def _flash_kernel(qa_ref, kt_ref, va_ref, o_ref, m_ref, acc_ref):
    qi = pl.program_id(1)
    ki = pl.program_id(2)
    nh = va_ref.shape[1] // LANES

    @pl.when(ki == 0)
    def _():
        m_ref[...] = jnp.full_like(m_ref, NEG_BIG)
        acc_ref[...] = jnp.zeros_like(acc_ref)

    def step(diag):
        for hh in range(nh):
            for qc in range(BQ // QC):
                rows = slice(qc * QC, (qc + 1) * QC)
                q = qa_ref[rows, hh * LANES:(hh + 1) * LANES]
                m = m_ref[hh, rows, :]
                acc = acc_ref[hh, rows, :]
                for kc in range(BK // KC):
                    if diag and kc * KC > qc * QC + QC - 1:
                        continue
                    cols = slice(kc * KC, (kc + 1) * KC)
                    s = jnp.dot(q, kt_ref[hh * LANES:(hh + 1) * LANES, cols],
                                preferred_element_type=F32)
                    if diag and kc * KC + KC - 1 > qc * QC:
                        r = lax.broadcasted_iota(jnp.int32, (QC, KC), 0) + qc * QC
                        c = lax.broadcasted_iota(jnp.int32, (QC, KC), 1) + kc * KC
                        s = jnp.where(c <= r, s, -jnp.inf)
                    parts = [s[:, j * LANES:(j + 1) * LANES] for j in range(KC // LANES)]
                    smax = functools.reduce(jnp.maximum, parts)
                    m_new = jnp.maximum(m, jnp.max(smax, axis=1, keepdims=True))
                    alpha = jnp.exp2(m - m_new)
                    pr = jnp.concatenate([jnp.exp2(pp - m_new) for pp in parts], axis=1)
                    acc = alpha * acc + jnp.dot(pr.astype(BF16),
                                                va_ref[cols, hh * LANES:(hh + 1) * LANES],
                                                preferred_element_type=F32)
                    m = m_new
                m_ref[hh, rows, :] = m
                acc_ref[hh, rows, :] = acc

    @pl.when(ki < qi)
    def _():
        step(False)

    @pl.when(ki == qi)
    def _():
        step(True)
        lane = lax.broadcasted_iota(jnp.int32, (BQ, LANES), 1)
        for j in range(nh // 2):
            a0 = acc_ref[2 * j]
            a1 = acc_ref[2 * j + 1]
            num = jnp.where(lane < HEAD_DIM, a0, pltpu.roll(a1, HEAD_DIM, axis=1))
            den = jnp.where(lane < HEAD_DIM, pltpu.roll(a0, HEAD_DIM, axis=1), a1)
            o_ref[:, j * LANES:(j + 1) * LANES] = (num / den).astype(o_ref.dtype)


def _flash(qa, kt, va, batch, seq):
    wide = qa.shape[1]
    db = wide // 2
    nh = db // HEAD_DIM
    nq = seq // BQ
    nk = seq // BK
    qrow = lambda b, qi, ki: (b * nq + qi, 0)
    in_specs = [
        pl.BlockSpec((BQ, wide), qrow),
        pl.BlockSpec((wide, BK), lambda b, qi, ki: (0, b * nk + jnp.minimum(ki, qi))),
        pl.BlockSpec((BK, wide), lambda b, qi, ki: (b * nk + jnp.minimum(ki, qi), 0)),
    ]
    return pl.pallas_call(
        _flash_kernel,
        grid=(batch, nq, nk), in_specs=in_specs,
        out_specs=pl.BlockSpec((BQ, db), qrow),
        out_shape=jax.ShapeDtypeStruct((batch * seq, db), BF16),
        scratch_shapes=[pltpu.VMEM((nh, BQ, LANES), F32), pltpu.VMEM((nh, BQ, LANES), F32)],
        compiler_params=_cparams(("arbitrary", "arbitrary", "arbitrary")), name="fox_prompt",
    )(qa, kt, va)


def _suffix_kernel(lf_ref, u2_ref, o_ref):
    o_ref[...] = jnp.dot(lf_ref[...], u2_ref[...], preferred_element_type=F32, precision=HIGHEST)


def _suffix(lf_rows, u2):
    n = lf_rows.shape[0]
    blk = math.gcd(n, SUFFIX_ROWS)
    return pl.pallas_call(
        _suffix_kernel, grid=(n // blk,),
        in_specs=[pl.BlockSpec((blk, PAGE), lambda i: (i, 0)), _full((PAGE, 2 * PAGE))],
        out_specs=pl.BlockSpec((blk, 2 * PAGE), lambda i: (i, 0)),
        out_shape=jax.ShapeDtypeStruct((n, 2 * PAGE), F32),
        compiler_params=_cparams(("arbitrary",)), name="logf_suffix",
    )(lf_rows, u2)


def _decode_kernel(n_steps, pt_ref, qbd_ref, kn_ref, vn_ref, lfn_ref, uincl_ref, *rest):
    del pt_ref
    g_pages = PAGES_PER_STEP
    kc_refs = rest[0:g_pages]
    vc_refs = rest[g_pages:2 * g_pages]
    ds_refs = rest[2 * g_pages:3 * g_pages]
    o_ref, m_ref, l_ref, acc_ref, carry_ref, cn_ref = rest[3 * g_pages:]
    p = pl.program_id(1)
    rows = qbd_ref.shape[1]
    nh = ds_refs[0].shape[1]
    s_new = rows // nh

    def expand(t):
        return jnp.concatenate(
            [jnp.broadcast_to(t[hh:hh + 1, :], (s_new, t.shape[1])) for hh in range(nh)], axis=0)

    @pl.when(p == 0)
    def _():
        m_ref[...] = jnp.full_like(m_ref, NEG_BIG)
        l_ref[...] = jnp.zeros_like(l_ref)
        acc_ref[...] = jnp.zeros_like(acc_ref)
        carry_ref[...] = jnp.zeros_like(carry_ref)
        cn_ref[...] = jnp.dot(lfn_ref[0], uincl_ref[...], preferred_element_type=F32,
                              precision=HIGHEST)

    q = qbd_ref[0]
    cnx = expand(cn_ref[...])
    lane = lax.broadcasted_iota(jnp.int32, (rows, PAGE), 1)
    tq = lax.broadcasted_iota(jnp.int32, (rows, PAGE), 0) % s_new
    cn_col = jnp.sum(jnp.where(lane == tq, cnx, 0.0), axis=1, keepdims=True)

    def attend(kts, vts, biases, keep):
        ss = [jnp.dot(q, kt, preferred_element_type=F32) + b for kt, b in zip(kts, biases)]
        if keep is not None:
            ss = [jnp.where(keep, s, -jnp.inf) for s in ss]
        smax = functools.reduce(jnp.maximum, ss)
        m_prev = m_ref[...]
        m_new = jnp.maximum(m_prev, jnp.max(smax, axis=1, keepdims=True))
        alpha = jnp.exp(m_prev - m_new)
        prs = [jnp.exp(s - m_new) for s in ss]
        psum = functools.reduce(jnp.add, prs)
        l_ref[...] = alpha * l_ref[...] + jnp.sum(psum, axis=1, keepdims=True)
        pvs = [lax.dot_general(pr.astype(BF16), vt, (((1,), (1,)), ((), ())),
                               preferred_element_type=F32) for pr, vt in zip(prs, vts)]
        acc_ref[...] = alpha * acc_ref[...] + functools.reduce(jnp.add, pvs)
        m_ref[...] = m_new

    carry = carry_ref[...]
    biases = []
    for g in range(g_pages):
        ds = ds_refs[g][0]
        biases.append(expand(ds[:, :PAGE] + carry) + cn_col)
        carry = carry + ds[:, PAGE:PAGE + 1]
    carry_ref[...] = carry
    attend([r[...].astype(BF16) for r in kc_refs], [r[...].astype(BF16) for r in vc_refs],
           biases, None)

    @pl.when(p == n_steps - 1)
    def _():
        attend([kn_ref[0]], [vn_ref[0]], [cn_col - cnx], lane <= tq)
        out = acc_ref[...] / l_ref[...]
        width = out.shape[1]
        r = lax.broadcasted_iota(jnp.int32, (rows, width), 0) // s_new
        cidx = lax.broadcasted_iota(jnp.int32, (rows, width), 1) // HEAD_DIM
        out = jnp.where(r == cidx, out, 0.0)
        o_ref[0] = jnp.sum(out.reshape(nh, s_new, width), axis=0).astype(o_ref.dtype)


def _decode(pt_flat, qbd, kn_pad, vn_pad, lfn_t, kc, vc, dsuf, uincl, n_pages):
    bd_, rows, db = qbd.shape
    nh = dsuf.shape[1]
    s_new = rows // nh
    g_pages = PAGES_PER_STEP
    n_steps = n_pages // g_pages
    seqb = lambda b, p, pt: (b, 0, 0)

    def page(g):
        return lambda b, p, pt: (pt[b * n_pages + (n_pages - 1 - (p * g_pages + g))], 0, 0)

    in_specs = [
        pl.BlockSpec((1, rows, db), seqb),
        pl.BlockSpec((1, db, PAGE), seqb),
        pl.BlockSpec((1, db, PAGE), seqb),
        pl.BlockSpec((1, nh, PAGE), seqb),
        pl.BlockSpec((PAGE, PAGE), lambda b, p, pt: (0, 0)),
    ]
    in_specs += [pl.BlockSpec((None, db, PAGE), page(g)) for g in range(g_pages)]
    in_specs += [pl.BlockSpec((None, db, PAGE), page(g)) for g in range(g_pages)]
    in_specs += [pl.BlockSpec((1, nh, 2 * PAGE), page(g)) for g in range(g_pages)]
    grid_spec = pltpu.PrefetchScalarGridSpec(
        num_scalar_prefetch=1, grid=(bd_, n_steps), in_specs=in_specs,
        out_specs=pl.BlockSpec((1, s_new, db), seqb),
        scratch_shapes=[pltpu.VMEM((rows, 1), F32), pltpu.VMEM((rows, 1), F32),
                        pltpu.VMEM((rows, db), F32), pltpu.VMEM((nh, 1), F32),
                        pltpu.VMEM((nh, PAGE), F32)],
    )
    return pl.pallas_call(
        functools.partial(_decode_kernel, n_steps), grid_spec=grid_spec,
        out_shape=jax.ShapeDtypeStruct((bd_, s_new, db), BF16),
        compiler_params=_cparams(("arbitrary", "arbitrary")), name="fox_decode",
    )(pt_flat, qbd, kn_pad, vn_pad, lfn_t, uincl, *([kc] * g_pages), *([vc] * g_pages),
      *([dsuf] * g_pages))


def _outproj_kernel(n_prompt_tiles, x_ref, oa_ref, obp_ref, obs_ref, woa_ref, wob_ref, g2_ref,
                    wr_ref, br_ref, x1_ref, h2_ref, ti_ref, tg_ref):
    i = pl.program_id(0)
    ob = jnp.where(i >= n_prompt_tiles, obs_ref[...], obp_ref[...])
    mix = (jnp.dot(oa_ref[...], woa_ref[...], preferred_element_type=F32)
           + jnp.dot(ob, wob_ref[...], preferred_element_type=F32))
    x1 = x_ref[...] + mix
    x1_ref[...] = x1
    ms = jnp.mean(x1 * x1, axis=-1, keepdims=True)
    h2 = x1 * lax.rsqrt(ms + EPS) * g2_ref[...]
    h2_ref[...] = h2
    h_hi = h2.astype(BF16)
    h_lo = (h2 - h_hi.astype(F32)).astype(BF16)
    wr = wr_ref[...]
    w_hi = wr.astype(BF16)
    w_lo = (wr - w_hi.astype(F32)).astype(BF16)
    logits = (jnp.dot(h_hi, w_hi, preferred_element_type=F32)
              + jnp.dot(h_hi, w_lo, preferred_element_type=F32)
              + jnp.dot(h_lo, w_hi, preferred_element_type=F32)) + br_ref[...]
    lane = lax.broadcasted_iota(jnp.int32, logits.shape, 1)
    cur = jnp.where(lane < N_EXPERTS, logits, -jnp.inf)
    idx_out = jnp.zeros(logits.shape, jnp.int32)
    val_out = jnp.zeros(logits.shape, F32)
    v0 = None
    denom = None
    for k in range(TOP_K):
        mval = jnp.max(cur, axis=1, keepdims=True)
        midx = jnp.min(jnp.where(cur == mval, lane, LANES), axis=1, keepdims=True)
        if k == 0:
            v0 = mval
            ek = jnp.ones_like(mval)
            denom = ek
        else:
            ek = jnp.exp(mval - v0)
            denom = denom + ek
        idx_out = jnp.where(lane == k, midx, idx_out)
        val_out = jnp.where(lane == k, ek, val_out)
        cur = jnp.where(lane == midx, -jnp.inf, cur)
    ti_ref[...] = idx_out
    tg_ref[...] = val_out / denom


def _outproj(x, oa, obp, obs, woa, wob, g2, wr, br, n_prompt_tiles):
    nt, d = x.shape
    half = oa.shape[1]
    row = lambda w: pl.BlockSpec((TM, w), lambda i: (i, 0))
    in_specs = [
        row(d), row(half),
        pl.BlockSpec((TM, half), lambda i: (jnp.minimum(i, n_prompt_tiles - 1), 0)),
        _full((TM, half)),
        _full((half, d)), _full((half, d)), _full((1, d)), _full((d, LANES)), _full((1, LANES)),
    ]
    out_shape = [jax.ShapeDtypeStruct((nt, d), F32), jax.ShapeDtypeStruct((nt, d), F32),
                 jax.ShapeDtypeStruct((nt, LANES), jnp.int32), jax.ShapeDtypeStruct((nt, LANES), F32)]
    return pl.pallas_call(
        functools.partial(_outproj_kernel, n_prompt_tiles),
        grid=(nt // TM,), in_specs=in_specs,
        out_specs=[row(d), row(d), row(LANES), row(LANES)], out_shape=out_shape,
        compiler_params=_cparams(("arbitrary",)), name="outproj_router",
    )(x, oa, obp, obs, woa, wob, g2, wr, br)


def _pack_pairs(x):
    n = x.shape[1] // 2
    lo = pltpu.bitcast(x[:, :n], jnp.uint32) >> 16
    hi = pltpu.bitcast(x[:, n:], jnp.uint32) & jnp.uint32(0xFFFF0000)
    return lo | hi


def _unpack_pairs(w):
    a = pltpu.bitcast(w << 16, F32)
    b = pltpu.bitcast(w & jnp.uint32(0xFFFF0000), F32)
    return jnp.concatenate([a, b], axis=1).astype(BF16)


def _dispatch_kernel(h2_ref, ti_ref, tg_ref, lstrict_ref, ustrict_ref, ones_ref,
                     xl_ref, pg_ref, cnt_ref, yl_ref):
    yl_ref[...] = jnp.zeros_like(yl_ref)
    rows = h2_ref.shape[0]
    rp = xl_ref.shape[0]
    half = h2_ref.shape[1] // 2
    lane = lax.broadcasted_iota(jnp.int32, (rows, LANES), 1)
    ti = ti_ref[...]
    tg = tg_ref[...]
    chose = [lane == ti[:, k:k + 1] for k in range(TOP_K)]
    member = functools.reduce(jnp.logical_or, chose)
    e_mat = jnp.where(member, 1.0, 0.0)
    rank = jnp.dot(lstrict_ref[...], e_mat.astype(BF16), preferred_element_type=F32)
    cnt = rank[rows - 1:, :] + e_mat[rows - 1:, :]
    gran = jnp.floor((cnt + (GRANULE - 1)) * (1.0 / GRANULE))
    off = jnp.dot(jnp.broadcast_to(gran * GRANULE, (8, LANES)).astype(BF16), ustrict_ref[...],
                  preferred_element_type=F32)[:1, :]
    pos = off + rank
    pk = pltpu.roll(tg, TOP_K, axis=1)
    for k in range(TOP_K):
        pos_k = jnp.sum(jnp.where(chose[k], pos, 0.0), axis=1, keepdims=True)
        pk = jnp.where(lane == k, pos_k, pk)
    pg_ref[...] = pk
    cnt_ref[...] = jnp.broadcast_to(gran, (8, LANES))

    pkt = pk.T
    r = lax.broadcasted_iota(jnp.int32, (rp, rows), 0).astype(F32)
    sel = jnp.zeros((rp, rows), F32)
    gsel = jnp.zeros((rp, rows), F32)
    for k in range(TOP_K):
        hit = r == pkt[k:k + 1, :]
        sel = jnp.where(hit, 1.0, sel)
        gsel = jnp.where(hit, pkt[TOP_K + k:TOP_K + k + 1, :], gsel)
    xl = jnp.dot(sel.astype(BF16), h2_ref[...].astype(BF16), preferred_element_type=F32)
    g_hi = gsel.astype(BF16)
    g_lo = (gsel - g_hi.astype(F32)).astype(BF16)
    gate = (jnp.dot(g_hi, ones_ref[...], preferred_element_type=F32)
            + jnp.dot(g_lo, ones_ref[...], preferred_element_type=F32))
    xl_ref[:, :half] = _pack_pairs(xl)
    xl_ref[:, half:] = pltpu.bitcast(gate, jnp.uint32)


def _dispatch(h2, ti, tg, lstrict, ustrict, ones):
    nt, d = h2.shape
    n_tiles = nt // TM
    rp = TOP_K * TM + N_EXPERTS * GRANULE
    row = lambda w: pl.BlockSpec((TM, w), lambda i: (i, 0))
    return pl.pallas_call(
        _dispatch_kernel, grid=(n_tiles,),
        in_specs=[row(d), row(LANES), row(LANES), _full((TM, TM)), _full((LANES, LANES)),
                  _full((TM, LANES))],
        out_specs=[pl.BlockSpec((rp, d // 2 + LANES), lambda i: (i, 0)), row(LANES),
                   pl.BlockSpec((8, LANES), lambda i: (i, 0)),
                   pl.BlockSpec((rp, d // 2), lambda i: (i, 0))],
        out_shape=[jax.ShapeDtypeStruct((n_tiles * rp, d // 2 + LANES), jnp.uint32),
                   jax.ShapeDtypeStruct((nt, LANES), F32),
                   jax.ShapeDtypeStruct((n_tiles * 8, LANES), F32),
                   jax.ShapeDtypeStruct((n_tiles * rp, d // 2), jnp.uint32)],
        compiler_params=_cparams(("arbitrary",)), name="dispatch",
    )(h2, ti, tg, lstrict, ustrict, ones)


def _granule_copies(idx_ref, base, hbm, buf, sem, to_hbm):
    copies = []
    for q in range(MOE_BLOCK // GRANULE):
        g = idx_ref[base + q]
        hrows = hbm.at[pl.ds(pl.multiple_of(g * GRANULE, GRANULE), GRANULE)]
        brows = buf.at[pl.ds(q * GRANULE, GRANULE)]
        copies.append(pltpu.make_async_copy(brows, hrows, sem) if to_hbm
                      else pltpu.make_async_copy(hrows, brows, sem))
    return copies


def _expert_kernel(be_ref, nv_ref, gidx_ref, xl_hbm, wgu_ref, bgu_ref, wdn_ref, bdn_ref,
                   yz_hbm, yl_hbm, xs_buf, y_buf, wgu_bf, wdn_bf, sem_in, sem_out):
    del yz_hbm
    i = pl.program_id(0)
    n_blocks = pl.num_programs(0)
    gpb = MOE_BLOCK // GRANULE
    d_ff = wdn_ref.shape[1]
    half = y_buf.shape[1]
    slot = i % 2

    def wait_in(s):
        pltpu.make_async_copy(xl_hbm.at[pl.ds(0, MOE_BLOCK)], xs_buf.at[s], sem_in.at[s]).wait()

    def for_out_copies(blk, fn):
        copies = _granule_copies(gidx_ref, blk * gpb, yl_hbm, y_buf, sem_out.at[0], True)
        n = nv_ref[blk]

        @pl.when(n == gpb)
        def _():
            for c in copies:
                fn(c)

        @pl.when(n < gpb)
        def _():
            for q, c in enumerate(copies):
                @pl.when(q < n)
                def _(c=c):
                    fn(c)

    @pl.when((i == 0) & (nv_ref[0] > 0))
    def _():
        for c in _granule_copies(gidx_ref, 0, xl_hbm, xs_buf.at[0], sem_in.at[0], False):
            c.start()

    @pl.when(nv_ref[i] > 0)
    def _():
        nxt = jnp.minimum(i + 1, n_blocks - 1)
        has_next = (i + 1 < n_blocks) & (nv_ref[nxt] > 0)

        @pl.when(has_next)
        def _():
            for c in _granule_copies(gidx_ref, nxt * gpb, xl_hbm, xs_buf.at[1 - slot],
                                     sem_in.at[1 - slot], False):
                c.start()

        prev = be_ref[jnp.maximum(i - 1, 0)]

        @pl.when((i == 0) | (prev != be_ref[i]))
        def _():
            wgu_bf[...] = wgu_ref[0].astype(BF16)
            wdn_bf[...] = wdn_ref[0].astype(BF16)

        wait_in(slot)
        xs = _unpack_pairs(xs_buf[slot, :, :half])
        gate = pltpu.bitcast(xs_buf[slot, :, half:], F32)
        gu = jnp.dot(xs, wgu_bf[...], preferred_element_type=F32) + bgu_ref[0]
        glu = jnp.minimum(gu[:, :d_ff], SWIGLU_LIMIT)
        lin = jnp.clip(gu[:, d_ff:], -SWIGLU_LIMIT, SWIGLU_LIMIT)
        act = glu * jax.nn.sigmoid(SWIGLU_ALPHA * glu) * (lin + 1.0)
        y = jnp.dot(act.astype(BF16), wdn_bf[...], preferred_element_type=F32) + bdn_ref[0]
        y = y * jnp.concatenate([gate] * (y.shape[1] // LANES), axis=1)
        packed = _pack_pairs(y.astype(BF16).astype(F32))

        @pl.when(i > 0)
        def _():
            for_out_copies(jnp.maximum(i - 1, 0), lambda c: c.wait())

        y_buf[...] = packed
        for_out_copies(i, lambda c: c.start())

        @pl.when(jnp.logical_not(has_next))
        def _():
            for_out_copies(i, lambda c: c.wait())


def _experts(block_e, block_nv, gidx, xl, yl_zero, wgu, bgu, wdn, bdn):
    n_blocks = block_e.shape[0]
    _, d, two_ff = wgu.shape
    d_ff = two_ff // 2
    eidx = lambda i, be, nv, gi: (be[i], 0, 0)
    grid_spec = pltpu.PrefetchScalarGridSpec(
        num_scalar_prefetch=3, grid=(n_blocks,),
        in_specs=[
            pl.BlockSpec(memory_space=pl.ANY),
            pl.BlockSpec((1, d, two_ff), eidx),
            pl.BlockSpec((1, 1, two_ff), eidx),
            pl.BlockSpec((1, d_ff, d), eidx),
            pl.BlockSpec((1, 1, d), eidx),
            pl.BlockSpec(memory_space=pl.ANY),
        ],
        out_specs=pl.BlockSpec(memory_space=pl.ANY),
        scratch_shapes=[pltpu.VMEM((2, MOE_BLOCK, xl.shape[1]), jnp.uint32),
                        pltpu.VMEM((MOE_BLOCK, d // 2), jnp.uint32),
                        pltpu.VMEM((d, two_ff), BF16), pltpu.VMEM((d_ff, d), BF16),
                        pltpu.SemaphoreType.DMA((2,)), pltpu.SemaphoreType.DMA((1,))],
    )
    return pl.pallas_call(
        _expert_kernel, grid_spec=grid_spec,
        out_shape=jax.ShapeDtypeStruct(yl_zero.shape, jnp.uint32),
        input_output_aliases={8: 0},
        compiler_params=_cparams(("arbitrary",)), name="experts",
    )(block_e, block_nv, gidx, xl, wgu, bgu, wdn, bdn, yl_zero)


def _combine_kernel(x1_ref, pg_ref, yl_ref, o_ref):
    rows = x1_ref.shape[0]
    rp = yl_ref.shape[0]
    pg = pg_ref[...]
    r = lax.broadcasted_iota(jnp.int32, (rows, rp), 1).astype(F32)
    sel = jnp.zeros((rows, rp), F32)
    for k in range(TOP_K):
        sel = jnp.where(r == pg[:, k:k + 1], 1.0, sel)
    yl = _unpack_pairs(yl_ref[...])
    o_ref[...] = x1_ref[...] + jnp.dot(sel.astype(BF16), yl, preferred_element_type=F32)


def _combine(x1, pg, yl):
    nt, d = x1.shape
    n_tiles = nt // TM
    rp = TOP_K * TM + N_EXPERTS * GRANULE
    row = lambda w: pl.BlockSpec((TM, w), lambda i: (i, 0))
    return pl.pallas_call(
        _combine_kernel, grid=(n_tiles,),
        in_specs=[row(d), row(LANES), pl.BlockSpec((rp, d // 2), lambda i: (i, 0))],
        out_specs=row(d),
        out_shape=jax.ShapeDtypeStruct((nt, d), F32),
        compiler_params=_cparams(("arbitrary",)), name="combine",
    )(x1, pg, yl)


def _route(cnt_rows, n_tiles, nt):
    i32 = jnp.int32
    cg = cnt_rows.reshape(n_tiles, 8, LANES)[:, 0, :N_EXPERTS].astype(i32)
    gp_tile = (TOP_K * TM + N_EXPERTS * GRANULE) // GRANULE
    gpb = MOE_BLOCK // GRANULE
    seg = jnp.arange(n_tiles, dtype=i32)[:, None] * gp_tile + jnp.cumsum(cg, axis=1) - cg
    cum_incl = jnp.cumsum(cg, axis=0)
    cum_excl = cum_incl - cg
    ng = cum_incl[-1]
    nb = (ng + gpb - 1) // gpb
    bend = jnp.cumsum(nb)
    bstart = bend - nb
    max_gran = (nt * TOP_K) // GRANULE + n_tiles * N_EXPERTS
    n_blocks = -(-max_gran // gpb) + N_EXPERTS
    bi = jnp.arange(n_blocks, dtype=i32)
    block_e = jnp.minimum(jnp.sum((bend[None, :] <= bi[:, None]).astype(i32), axis=1),
                          N_EXPERTS - 1).astype(i32)
    block_valid = (bi < bend[-1]).astype(i32)
    onehot = (block_e[:, None] == jnp.arange(N_EXPERTS, dtype=i32)[None, :]).astype(F32)
    pick = lambda table: jnp.dot(onehot, table.astype(F32), precision=HIGHEST)
    cum_rows = pick(cum_incl.T)
    base_rows = pick((seg - cum_excl).T)
    u0 = (bi.astype(F32) - pick(bstart)) * gpb
    q = jnp.arange(gpb, dtype=i32)
    u = u0[:, None] + q[None, :].astype(F32)
    valid = (u < pick(ng)[:, None]) & (block_valid[:, None] == 1)
    tile_of = jnp.minimum(jnp.sum((cum_rows[:, None, :] <= u[:, :, None]).astype(i32), axis=2),
                          n_tiles - 1)
    in_tile = tile_of[:, :, None] == jnp.arange(n_tiles, dtype=i32)[None, None, :]
    g = (jnp.sum(jnp.where(in_tile, base_rows[:, None, :], 0.0), axis=2) + u).astype(i32)
    gidx = jnp.where(valid, g, 0).reshape(-1).astype(i32)
    block_nv = jnp.sum(valid.astype(i32), axis=1)
    return block_e, block_nv, gidx


def _bias_selector(nh, db):
    rows = jnp.arange(LANES)[:, None]
    cols = jnp.arange(2 * db)[None, :]
    head = cols // LANES
    off = cols % LANES - BIAS_LANE
    part = rows // nh
    is_split = (part < N_SPLIT) & (rows % nh == head)
    is_one = rows == N_SPLIT * nh
    q_sel = ((is_split & (off == part)) | (is_one & (off >= N_SPLIT) & (off < 2 * N_SPLIT)))
    k_sel = (is_one & (off >= 0) & (off < N_SPLIT)).astype(F32) \
        - (is_split & (off == part + N_SPLIT)).astype(F32)
    return jnp.concatenate([q_sel.astype(F32), k_sel], axis=1).astype(BF16)


def kernel(x_prompt, x_sample, cache_k, cache_v, cache_logf, page_table, norm1_g, w_in, b_f, gv_g,
           w_s, b_s, q_g, k_g, w_out, norm2_g, w_router, b_router, w_gu, b_gu, w_dn, b_dn):
    batch, seq, d = x_prompt.shape
    dec_b, dec_s, _ = x_sample.shape
    depth = w_in.shape[0]
    ha = gv_g.shape[1]
    hb = b_f.shape[1]
    da, db = ha * HEAD_DIM, hb * HEAD_DIM
    n_main = 2 * da + 3 * db
    n_phys = cache_k.shape[1]
    n_pages = page_table.shape[1]
    t_prompt = batch * seq
    t_dec = dec_b * dec_s
    assert t_dec == TM and seq % TM == 0 and TM % CHUNK == 0 and seq % BQ == 0 and BQ == BK
    assert da % LANES == 0 and db % LANES == 0 and dec_s <= CHUNK and TM % dec_s == 0
    assert cache_k.shape[2] == PAGE and n_pages % PAGES_PER_STEP == 0
    assert (N_SPLIT + 1) * hb <= LANES
    n_prompt_tiles = t_prompt // TM
    tiles_per_seq = seq // TM

    x = jnp.concatenate([x_prompt.reshape(t_prompt, d), x_sample.reshape(t_dec, d)], axis=0)

    wm_all = w_in[:, :, :n_main].astype(BF16)
    wf_all = jnp.pad(w_in[:, :, n_main:], ((0, 0), (0, 0), (0, LANES - hb))).astype(BF16)
    bf_all = jnp.pad(b_f, ((0, 0), (0, LANES - hb))).reshape(depth, 1, LANES)
    gv_all = gv_g.reshape(depth, 1, da)
    qg_all = jnp.tile(q_g, (1, hb)).reshape(depth, 1, db)
    kg_all = jnp.tile(k_g, (1, hb)).reshape(depth, 1, db)
    head_of = jnp.arange(db) // HEAD_DIM
    bd = (head_of[:, None] == head_of[None, :]).astype(BF16)
    tril = jnp.tril(jnp.ones((CHUNK, CHUNK), F32))
    wt = w_s * tril
    eye_p = jnp.eye(TM // CHUNK, dtype=F32)
    eye_s = jnp.eye(TM // dec_s, dtype=F32)
    wblk_p = jnp.einsum('ab,lhts->lhatbs', eye_p, wt).reshape(depth, ha, TM, TM)
    wblk_s = jnp.einsum('ab,lhts->lhatbs', eye_s, wt[:, :, :dec_s, :dec_s]).reshape(depth, ha, TM, TM)
    wblk_all = jnp.stack([wblk_p, wblk_s], axis=1).astype(BF16)
    bs_t = jnp.repeat(jnp.swapaxes(b_s, 1, 2), HEAD_DIM, axis=2)
    bias_p = jnp.tile(bs_t, (1, TM // CHUNK, 1))
    bias_s = jnp.tile(bs_t[:, :dec_s], (1, TM // dec_s, 1))
    bias_all = jnp.stack([bias_p, bias_s], axis=1)
    ltri = jnp.tril(jnp.ones((TM, TM), F32))
    sel = _bias_selector(hb, db)
    woa_all = w_out[:, :da].astype(BF16)
    wob_all = w_out[:, da:].astype(BF16)
    wr_all = jnp.pad(w_router, ((0, 0), (0, 0), (0, LANES - N_EXPERTS)))
    br_all = jnp.pad(b_router, ((0, 0), (0, LANES - N_EXPERTS))).reshape(depth, 1, LANES)
    two_ff = w_gu.shape[3]
    wgu_all = w_gu.reshape(depth * N_EXPERTS, d, two_ff)
    bgu_all = b_gu.reshape(depth * N_EXPERTS, 1, two_ff)
    wdn_all = w_dn.reshape(depth * N_EXPERTS, two_ff // 2, d)
    bdn_all = b_dn.reshape(depth * N_EXPERTS, 1, d)
    nt = t_prompt + t_dec
    n_tiles = nt // TM
    ar_t = jnp.arange(TM)
    lstrict = (ar_t[:, None] > ar_t[None, :]).astype(BF16)
    ar_l = jnp.arange(LANES)
    ustrict_e = (ar_l[:, None] < ar_l[None, :]).astype(BF16)
    ones_tm = jnp.ones((TM, LANES), BF16)

    kc = jnp.transpose(cache_k, (0, 1, 3, 4, 2)).reshape(depth * n_phys, db, PAGE)
    vc = jnp.transpose(cache_v, (0, 1, 3, 4, 2)).reshape(depth * n_phys, db, PAGE)
    lfc_rows = jnp.swapaxes(cache_logf, 2, 3).reshape(depth * n_phys * hb, PAGE)
    pt = page_table.reshape(dec_b * n_pages).astype(jnp.int32)
    ar = jnp.arange(PAGE)
    ustrict = (ar[:, None] > ar[None, :]).astype(F32)
    uincl = (ar[:, None] <= ar[None, :]).astype(F32)
    u2 = jnp.concatenate([ustrict, jnp.ones((PAGE, PAGE), F32)], axis=1)
    dsuf = _suffix(lfc_rows, u2).reshape(depth * n_phys, hb, 2 * PAGE)
    eye_h = jnp.eye(hb, dtype=F32)

    kstack = jnp.zeros((depth, batch, db, seq), F32)
    vstack = jnp.zeros((depth, batch, db, seq), F32)
    lstack = jnp.zeros((depth, batch, hb, seq), F32)

    ks_l, vs_l, fs_l, cv_l = [], [], [], []
    for l in range(depth):
        oa, qa, kt, va, kstack, vstack, lstack, va_s, q_s, k_s, v_s, lf_s = _project(
            l, x, norm1_g[l].reshape(1, d), wm_all[l], wf_all[l], bf_all[l], gv_all[l], qg_all[l],
            kg_all[l], bd, wblk_all[l], bias_all[l], ltri, sel, kstack, vstack, lstack,
            n_prompt_tiles, tiles_per_seq)

        ob_p = _flash(qa, kt, va, batch, seq)

        q4 = q_s.reshape(dec_b, dec_s, hb, HEAD_DIM) * (HEAD_DIM ** -0.5)
        qbd = jnp.einsum('bthd,hg->bhtgd', q4, eye_h).reshape(dec_b, hb * dec_s, db).astype(BF16)
        k_new = k_s.reshape(dec_b, dec_s, db)
        v_new = v_s.reshape(dec_b, dec_s, db)
        lf_new = lf_s.reshape(dec_b, dec_s, hb)
        padr = ((0, 0), (0, 0), (0, PAGE - dec_s))
        kn_pad = jnp.pad(jnp.swapaxes(k_new, 1, 2), padr).astype(BF16)
        vn_pad = jnp.pad(jnp.swapaxes(v_new, 1, 2), padr).astype(BF16)
        lfn_t = jnp.pad(jnp.swapaxes(lf_new, 1, 2), ((0, 0), (0, 0), (0, PAGE - dec_s)))
        ob_s = _decode(pt + l * n_phys, qbd, kn_pad, vn_pad, lfn_t, kc, vc, dsuf, uincl,
                       n_pages).reshape(t_dec, db)

        x1, h2, ti, tg = _outproj(x, oa, ob_p, ob_s, woa_all[l], wob_all[l],
                                  norm2_g[l].reshape(1, d), wr_all[l], br_all[l], n_prompt_tiles)
        xl, pg, cnt_rows, yl_init = _dispatch(h2, ti, tg, lstrict, ustrict_e, ones_tm)
        block_e, block_nv, gidx = _route(cnt_rows, n_tiles, nt)
        yl = _experts(block_e + l * N_EXPERTS, block_nv, gidx, xl, yl_init, wgu_all,
                      bgu_all, wdn_all, bdn_all)
        x = _combine(x1, pg, yl)

        ks_l.append(k_new.reshape(dec_b, dec_s, hb, HEAD_DIM))
        vs_l.append(v_new.reshape(dec_b, dec_s, hb, HEAD_DIM))
        fs_l.append(lf_new)
        cv_l.append(va_s.reshape(dec_b, dec_s, ha, HEAD_DIM))

    y_prompt = x[:t_prompt].reshape(batch, seq, d)
    y_sample = x[t_prompt:].reshape(dec_b, dec_s, d)
    k_prompt = jnp.transpose(kstack.reshape(depth, batch, hb, HEAD_DIM, seq), (0, 1, 4, 2, 3))
    v_prompt = jnp.transpose(vstack.reshape(depth, batch, hb, HEAD_DIM, seq), (0, 1, 4, 2, 3))
    logf_prompt = jnp.transpose(lstack, (0, 1, 3, 2))
    return (y_prompt, y_sample, k_prompt, v_prompt, logf_prompt,
            jnp.stack(ks_l), jnp.stack(vs_l), jnp.stack(fs_l), jnp.stack(cv_l))
```
